```python
import math
import jax, jax.numpy as jnp
from jax import lax
import numpy as np

D_MODEL = 1024
BATCH = 8
SEQ = 4096
DEPTH = 1

EPS = 1e-6
BLOCK_Q = 128
PLE_DIM = 256
MLA_HEADS = 8
MLA_NOPE = 64
MLA_ROPE = 32
MLA_V = 64
MLA_Q_RANK = 384
MLA_KV_RANK = 256
ROPE_THETA = 10000.0
MLA_SCALE = 1.0 / math.sqrt(MLA_NOPE + MLA_ROPE)
SB_HEADS = 8
SB_DIM = 64
SB_SCALE = 1.0 / math.sqrt(SB_DIM)
D_FF = ((8 * D_MODEL // 3 + 255) // 256) * 256
IN_WIDTHS = (MLA_Q_RANK, MLA_KV_RANK, MLA_ROPE,
             SB_HEADS * SB_DIM, SB_HEADS * SB_DIM, SB_HEADS * SB_DIM,
             D_MODEL, D_MODEL)
D_IN = sum(IN_WIDTHS)
NEG_INF = -1e30

kernel_name = "hybrid_mla_stickbreaking_gated_block"


def rms_norm(x, g):
    x32 = x.astype(jnp.float32)
    y = x32 * lax.rsqrt(jnp.mean(x32 * x32, axis=-1, keepdims=True) + EPS)
    return (y * g.astype(jnp.float32)).astype(x.dtype)


def split_cols(t, widths):
    outs, start = [], 0
    for w in widths:
        outs.append(t[..., start:start + w])
        start += w
    return outs


def rope_tables(positions, dtype):
    inv_freq = 1.0 / (ROPE_THETA ** (jnp.arange(0, MLA_ROPE, 2, dtype=jnp.float32) / MLA_ROPE))
    ang = positions.astype(jnp.float32)[..., None] * inv_freq
    return jnp.cos(ang).astype(dtype), jnp.sin(ang).astype(dtype)


def apply_rope(t, cos, sin):
    half = t.shape[-1] // 2
    t1, t2 = t[..., :half], t[..., half:]
    return jnp.concatenate([t1 * cos - t2 * sin, t1 * sin + t2 * cos], axis=-1)


def to_blocks(t):
    B, S, H, d = t.shape
    return t.reshape(B, S // BLOCK_Q, BLOCK_Q, H, d).transpose(1, 0, 3, 2, 4)


def from_blocks(o):
    nb, B, H, Q, d = o.shape
    return o.transpose(1, 0, 3, 2, 4).reshape(B, nb * Q, H * d)


def mla_attention(q_nope, q_pe, k_nope, k_pe, v):
    S = q_nope.shape[1]
    nb = S // BLOCK_Q
    qn, qp = to_blocks(q_nope), to_blocks(q_pe)
    kn = k_nope.transpose(0, 2, 1, 3)
    vv = v.transpose(0, 2, 1, 3)
    kpos = jnp.arange(S)

    def step(args):
        qn_b, qp_b, blk = args
        qpos = blk * BLOCK_Q + jnp.arange(BLOCK_Q)
        s = (jnp.einsum('bhqd,bhkd->bhqk', qn_b, kn)
             + jnp.einsum('bhqr,bkr->bhqk', qp_b, k_pe)).astype(jnp.float32) * MLA_SCALE
        s = jnp.where(kpos[None, :] <= qpos[:, None], s, NEG_INF)
        w = jax.nn.softmax(s, axis=-1).astype(vv.dtype)
        return jnp.einsum('bhqk,bhkd->bhqd', w, vv)

    return from_blocks(lax.map(step, (qn, qp, jnp.arange(nb))))


def stick_breaking_attention(q, k, v):
    S = q.shape[1]
    nb = S // BLOCK_Q
    qb = to_blocks(q)
    kk = k.transpose(0, 2, 1, 3)
    vv = v.transpose(0, 2, 1, 3)
    kpos = jnp.arange(S)

    def step(args):
        q_b, blk = args
        qpos = blk * BLOCK_Q + jnp.arange(BLOCK_Q)
        causal = kpos[None, :] < qpos[:, None]
        z = jnp.einsum('bhqd,bhkd->bhqk', q_b, kk).astype(jnp.float32) * SB_SCALE
        log_1m = jnp.where(causal, jax.nn.log_sigmoid(-z), 0.0)
        after = lax.cumsum(log_1m, axis=3, reverse=True) - log_1m
        a = jnp.where(causal, jnp.exp(jax.nn.log_sigmoid(z) + after), 0.0)
        return jnp.einsum('bhqk,bhkd->bhqd', a.astype(vv.dtype), vv)

    return from_blocks(lax.map(step, (qb, jnp.arange(nb))))


def setup_inputs(seed: int = 0) -> dict:
    key = jax.random.key(seed)
    ks = jax.random.split(key, 24)
    f32 = jnp.float32

    def w(k, shape):
        return jax.random.normal(k, shape, f32) * (shape[-2] ** -0.5)

    def gain(k, shape):
        return 1.0 + 0.01 * jax.random.normal(k, shape, f32)

    L = DEPTH
    return {
        "x": jax.random.normal(ks[0], (BATCH, SEQ, D_MODEL), f32),
        "p": jax.random.normal(ks[1], (DEPTH, BATCH, SEQ, PLE_DIM), f32),
        "positions": jnp.broadcast_to(jnp.arange(SEQ, dtype=jnp.int32), (BATCH, SEQ)),
        "g_mix": gain(ks[2], (L, D_MODEL)),
        "w_in": w(ks[3], (L, D_MODEL, D_IN)),
        "g_q_a": gain(ks[4], (L, MLA_Q_RANK)),
        "w_q_b": w(ks[5], (L, MLA_Q_RANK, MLA_HEADS * (MLA_NOPE + MLA_ROPE))),
        "g_kv_a": gain(ks[6], (L, MLA_KV_RANK)),
        "w_kv_b": w(ks[7], (L, MLA_KV_RANK, MLA_HEADS * (MLA_NOPE + MLA_V))),
        "w_br_mla": w(ks[8], (L, MLA_HEADS * MLA_V, D_MODEL)),
        "w_br_sb": w(ks[9], (L, SB_HEADS * SB_DIM, D_MODEL)),
        "w_out": w(ks[10], (L, D_MODEL, D_MODEL)),
        "g_ffn": gain(ks[11], (L, D_MODEL)),
        "w_ffn_gate": w(ks[12], (L, D_MODEL, D_FF)),
        "w_ffn_up": w(ks[13], (L, D_MODEL, D_FF)),
        "w_ffn_down": w(ks[14], (L, D_FF, D_MODEL)),
        "w_ple_gate": w(ks[15], (L, D_MODEL, D_MODEL)),
        "w_ple_proj": w(ks[16], (L, PLE_DIM, D_MODEL)),
        "g_ple": gain(ks[17], (L, D_MODEL)),
        "g_final": gain(ks[18], (D_MODEL,)),
    }


def reference(x, p, positions, g_mix, w_in, g_q_a, w_q_b, g_kv_a, w_kv_b, w_br_mla, w_br_sb,
              w_out, g_ffn, w_ffn_gate, w_ffn_up, w_ffn_down, w_ple_gate, w_ple_proj, g_ple,
              g_final):
    B, S, _ = x.shape
    cos, sin = rope_tables(positions, x.dtype)
    h = x
    for i in range(DEPTH):
        n = rms_norm(h, g_mix[i])
        proj = n @ w_in[i]
        c_q, c_kv, k_pe, q_sb, k_sb, v_sb, gate_a, gate_b = split_cols(proj, IN_WIDTHS)

        q = (rms_norm(c_q, g_q_a[i]) @ w_q_b[i]).reshape(B, S, MLA_HEADS, MLA_NOPE + MLA_ROPE)
        q_nope, q_pe = q[..., :MLA_NOPE], q[..., MLA_NOPE:]
        q_pe = apply_rope(q_pe, cos[:, :, None, :], sin[:, :, None, :])
        kv = (rms_norm(c_kv, g_kv_a[i]) @ w_kv_b[i]).reshape(B, S, MLA_HEADS, MLA_NOPE + MLA_V)
        k_nope, v_mla = kv[..., :MLA_NOPE], kv[..., MLA_NOPE:]
        k_pe = apply_rope(k_pe, cos, sin)
        o_a = mla_attention(q_nope, q_pe, k_nope, k_pe, v_mla)

        o_b = stick_breaking_attention(q_sb.reshape(B, S, SB_HEADS, SB_DIM),
                                       k_sb.reshape(B, S, SB_HEADS, SB_DIM),
                                       v_sb.reshape(B, S, SB_HEADS, SB_DIM))

        merged = (jax.nn.sigmoid(gate_a) * (o_a @ w_br_mla[i])
                  + jax.nn.sigmoid(gate_b) * (o_b @ w_br_sb[i]))
        h = h + merged @ w_out[i]

        n2 = rms_norm(h, g_ffn[i])
        h = h + (jax.nn.silu(n2 @ w_ffn_gate[i]) * (n2 @ w_ffn_up[i])) @ w_ffn_down[i]

        e = rms_norm(p[i] @ w_ple_proj[i], g_ple[i])
        h = h + jax.nn.sigmoid(h @ w_ple_gate[i]) * e
    return rms_norm(h, g_final)
```

```python
import functools
import math

import jax
import jax.numpy as jnp
from jax import lax
from jax.experimental import pallas as pl
from jax.experimental.pallas import tpu as pltpu

EPS = 1e-6
MLA_HEADS = 8
MLA_NOPE = 64
MLA_ROPE = 32
MLA_V = 64
MLA_Q_RANK = 384
MLA_KV_RANK = 256
ROPE_THETA = 10000.0
MLA_SCALE = 1.0 / math.sqrt(MLA_NOPE + MLA_ROPE)
SB_HEADS = 8
SB_DIM = 64
SB_SCALE = 1.0 / math.sqrt(SB_DIM)
NEG_INF = -1e30

LANES = 128
HEAD_PAIR = 2
VMEM_LIMIT = 56 * 1024 * 1024

F32 = jnp.float32
BF16 = jnp.bfloat16


def _rms(x, g):
    return x * lax.rsqrt(jnp.mean(x * x, axis=-1, keepdims=True) + EPS) * g


def _dot(a, b):
    return jnp.dot(a, b, preferred_element_type=F32)


def _dot_nt(a, b):
    return lax.dot_general(a, b, (((1,), (1,)), ((), ())), preferred_element_type=F32)


def _proj_kernel(x_ref, pos_ref, freq_ref, g_mix_ref, wa_ref, wsb_ref, gq_ref, wq_ref, wqr_ref,
                 gkv_ref, wk_ref, wv_ref,
                 qm_ref, km_ref, vm_ref, qs_ref, ks_ref, vs_ref):
    n = _rms(x_ref[...], g_mix_ref[...]).astype(BF16)
    pa = _dot(n, wa_ref[...])
    c_q = pa[:, :MLA_Q_RANK]
    c_kv = pa[:, MLA_Q_RANK:MLA_Q_RANK + MLA_KV_RANK]
    kpe = pa[:, 640:768]
    kpe_rot = pa[:, 768:896]

    ang = pos_ref[...].astype(F32) * freq_ref[...]
    cos_t = jnp.cos(ang)
    sin_t = jnp.sin(ang)

    nq = _rms(c_q, gq_ref[...]).astype(BF16)
    q = _dot(nq, wq_ref[...])
    q_rot = _dot(nq, wqr_ref[...])
    nkv = _rms(c_kv, gkv_ref[...]).astype(BF16)
    kn = _dot(nkv, wk_ref[...])
    k_pe = kpe * cos_t + kpe_rot * sin_t
    for h in range(MLA_HEADS):
        sl = slice(h * LANES, (h + 1) * LANES)
        qm_ref[:, sl] = ((q[:, sl] * cos_t + q_rot[:, sl] * sin_t) * MLA_SCALE).astype(BF16)
        km_ref[:, sl] = (kn[:, sl] + k_pe).astype(BF16)
    vm_ref[...] = _dot(nkv, wv_ref[...]).astype(BF16)

    sb = _dot(n, wsb_ref[...])
    w = SB_HEADS * SB_DIM
    qs_ref[...] = (sb[:, :w] * SB_SCALE).astype(BF16)
    ks_ref[...] = sb[:, w:2 * w].astype(BF16)
    vs_ref[...] = sb[:, 2 * w:].astype(BF16)


def _const_spec(shape):
    return pl.BlockSpec(shape, lambda *_: (0,) * len(shape), pipeline_mode=pl.Buffered(1))


def _proj_call(x2, pos2, freq, g_mix, wa, wsb, gq, wq, wqr, gkv, wk, wv, *, tm):
    T, D = x2.shape
    row = lambda w: pl.BlockSpec((tm, w), lambda i: (i, 0))
    consts = (freq, g_mix, wa, wsb, gq, wq, wqr, gkv, wk, wv)
    out_w = (MLA_HEADS * LANES, MLA_HEADS * LANES, MLA_HEADS * MLA_V,
             SB_HEADS * SB_DIM, SB_HEADS * SB_DIM, SB_HEADS * SB_DIM)
    return pl.pallas_call(
        _proj_kernel,
        grid=(T // tm,),
        in_specs=[row(D), row(1)] + [_const_spec(c.shape) for c in consts],
        out_specs=[row(w) for w in out_w],
        out_shape=[jax.ShapeDtypeStruct((T, w), BF16) for w in out_w],
        compiler_params=pltpu.CompilerParams(dimension_semantics=("parallel",),
                                             vmem_limit_bytes=VMEM_LIMIT),
        name="proj",
    )(x2, pos2, *consts)


def _mla_kernel(q_ref, k_ref, v_ref, o_ref, acc_ref, m_ref, *, tq, tk):
    qi = pl.program_id(2)
    acc_ref[...] = jnp.zeros_like(acc_ref)
    m_ref[...] = jnp.full_like(m_ref, NEG_INF)
    v_lane = lax.broadcasted_iota(jnp.int32, (tk, LANES), 1)
    row_g = qi * tq + lax.broadcasted_iota(jnp.int32, (tq, tk), 0)
    col_l = lax.broadcasted_iota(jnp.int32, (tq, tk), 1)

    def step(j, masked):
        k0 = pl.multiple_of(j * tk, tk)
        v_blk = v_ref[pl.ds(k0, tk), :]
        for h in range(HEAD_PAIR):
            sl = slice(h * LANES, (h + 1) * LANES)
            s = _dot_nt(q_ref[:, sl], k_ref[pl.ds(k0, tk), sl])
            if masked:
                s = jnp.where(col_l + k0 <= row_g, s, NEG_INF)
            m_prev = m_ref[h]
            m_new = jnp.maximum(m_prev, jnp.max(s, axis=1, keepdims=True))
            p = jnp.exp(s - m_new)
            own = (v_lane < MLA_V) if h == 0 else (v_lane >= MLA_V)
            v_aug = jnp.where(own, v_blk, jnp.ones_like(v_blk))
            acc_ref[h] = jnp.exp(m_prev - m_new) * acc_ref[h] + _dot(p.astype(BF16), v_aug)
            m_ref[h] = m_new

    n_full = (qi * tq) // tk
    lax.fori_loop(0, n_full, lambda j, c: (step(j, False), c)[1], 0)
    for d in range(max(1, tq // tk)):
        step(n_full + d, True)

    o_lane = lax.broadcasted_iota(jnp.int32, (tq, LANES), 1)
    a0, a1 = acc_ref[0], acc_ref[1]
    o0 = a0 / pltpu.roll(a0, MLA_V, 1)
    o1 = a1 / pltpu.roll(a1, MLA_V, 1)
    o_ref[...] = jnp.where(o_lane < MLA_V, o0, o1).astype(o_ref.dtype)


def _mla_call(qm, km, vm, *, tq, tk):
    B, S, _ = qm.shape
    n_pairs = MLA_HEADS // HEAD_PAIR
    return pl.pallas_call(
        functools.partial(_mla_kernel, tq=tq, tk=tk),
        grid=(B, n_pairs, S // tq),
        in_specs=[pl.BlockSpec((None, tq, HEAD_PAIR * LANES), lambda b, hp, qi: (b, qi, hp)),
                  pl.BlockSpec((None, S, HEAD_PAIR * LANES), lambda b, hp, qi: (b, 0, hp)),
                  pl.BlockSpec((None, S, LANES), lambda b, hp, qi: (b, 0, hp))],
        out_specs=pl.BlockSpec((None, tq, LANES), lambda b, hp, qi: (b, qi, hp)),
        out_shape=jax.ShapeDtypeStruct((B, S, MLA_HEADS * MLA_V), BF16),
        scratch_shapes=[pltpu.VMEM((HEAD_PAIR, tq, LANES), F32),
                        pltpu.VMEM((HEAD_PAIR, tq, 1), F32)],
        compiler_params=pltpu.CompilerParams(dimension_semantics=("parallel", "parallel", "arbitrary"),
                                             vmem_limit_bytes=VMEM_LIMIT),
        name="mla_attn",
    )(qm, km, vm)


def _sb_kernel(q_ref, k_ref, v_ref, tri_ref, o_ref, acc_ref, carry_ref, qh_ref, *, tq, tk):
    qi = pl.program_id(2)
    acc_ref[...] = jnp.zeros_like(acc_ref)
    carry_ref[...] = jnp.zeros_like(carry_ref)
    q_lane = lax.broadcasted_iota(jnp.int32, (tq, LANES), 1)
    q = q_ref[...]
    qh_ref[0] = jnp.where(q_lane < SB_DIM, q, jnp.zeros_like(q))
    qh_ref[1] = jnp.where(q_lane >= SB_DIM, q, jnp.zeros_like(q))
    row_g = qi * tq + lax.broadcasted_iota(jnp.int32, (tq, tk), 0)
    col_l = lax.broadcasted_iota(jnp.int32, (tq, tk), 1)

    def step(j, masked):
        k0 = pl.multiple_of(j * tk, tk)
        k_blk = k_ref[pl.ds(k0, tk), :]
        v_blk = v_ref[pl.ds(k0, tk), :]
        for h in range(HEAD_PAIR):
            z = _dot_nt(qh_ref[h], k_blk)
            sp = jnp.maximum(z, 0.0) + jnp.log(1.0 + jnp.exp(-jnp.abs(z)))
            if masked:
                valid = col_l + k0 < row_g
                sp = jnp.where(valid, sp, 0.0)
            hi = sp.astype(BF16)
            lo = (sp - hi.astype(F32)).astype(BF16)
            tri = tri_ref[...]
            c = _dot(hi, tri) + _dot(lo, tri)
            carry = carry_ref[h]
            a = jnp.exp(z + (carry - c))
            if masked:
                a = jnp.where(valid, a, 0.0)
            acc_ref[h] += _dot(a.astype(BF16), v_blk)
            carry_ref[h] = carry - c[:, 0:1]

    n_full = (qi * tq) // tk
    n_mask = max(1, tq // tk)
    for d in reversed(range(n_mask)):
        step(n_full + d, True)
    lax.fori_loop(0, n_full, lambda i, c: (step(n_full - 1 - i, False), c)[1], 0)

    o_ref[...] = jnp.where(q_lane < SB_DIM, acc_ref[0], acc_ref[1]).astype(o_ref.dtype)


def _sb_call(qs, ks, vs, *, tq, tk):
    B, S, _ = qs.shape
    n_pairs = SB_HEADS // HEAD_PAIR
    r = lax.broadcasted_iota(jnp.int32, (tk, tk), 0)
    c = lax.broadcasted_iota(jnp.int32, (tk, tk), 1)
    tri = (r >= c).astype(BF16)
    return pl.pallas_call(
        functools.partial(_sb_kernel, tq=tq, tk=tk),
        grid=(B, n_pairs, S // tq),
        in_specs=[pl.BlockSpec((None, tq, LANES), lambda b, hp, qi: (b, qi, hp)),
                  pl.BlockSpec((None, S, LANES), lambda b, hp, qi: (b, 0, hp)),
                  pl.BlockSpec((None, S, LANES), lambda b, hp, qi: (b, 0, hp)),
                  pl.BlockSpec((tk, tk), lambda b, hp, qi: (0, 0))],
        out_specs=pl.BlockSpec((None, tq, LANES), lambda b, hp, qi: (b, qi, hp)),
        out_shape=jax.ShapeDtypeStruct((B, S, SB_HEADS * SB_DIM), BF16),
        scratch_shapes=[pltpu.VMEM((HEAD_PAIR, tq, LANES), F32),
                        pltpu.VMEM((HEAD_PAIR, tq, 1), F32),
                        pltpu.VMEM((HEAD_PAIR, tq, LANES), BF16)],
        compiler_params=pltpu.CompilerParams(dimension_semantics=("parallel", "parallel", "arbitrary"),
                                             vmem_limit_bytes=VMEM_LIMIT),
        name="sb_attn",
    )(qs, ks, vs, tri)


def _post_kernel(x_ref, oa_ref, ob_ref, p_ref, g_mix_ref, wga_ref, wgb_ref, wbra_ref, wbrb_ref, wout_ref,
                 g_ffn_ref, wfg_ref, wfu_ref, wfd_ref, wpg_ref, wpp_ref, g_ple_ref, g_fin_ref, out_ref,
                 *, ff_chunk, final_norm):
    x = x_ref[...]
    n = _rms(x, g_mix_ref[...]).astype(BF16)
    merged = (jax.nn.sigmoid(_dot(n, wga_ref[...])) * _dot(oa_ref[...], wbra_ref[...])
              + jax.nn.sigmoid(_dot(n, wgb_ref[...])) * _dot(ob_ref[...], wbrb_ref[...]))
    h = x + _dot(merged.astype(BF16), wout_ref[...])

    n2 = _rms(h, g_ffn_ref[...]).astype(BF16)
    d_ff = wfg_ref.shape[1]
    ff = None
    for c0 in range(0, d_ff, ff_chunk):
        g = _dot(n2, wfg_ref[:, c0:c0 + ff_chunk])
        u = _dot(n2, wfu_ref[:, c0:c0 + ff_chunk])
        part = _dot((g * jax.nn.sigmoid(g) * u).astype(BF16), wfd_ref[c0:c0 + ff_chunk, :])
        ff = part if ff is None else ff + part
    h = h + ff

    e = _rms(_dot(p_ref[...].astype(BF16), wpp_ref[...]), g_ple_ref[...])
    h = h + jax.nn.sigmoid(_dot(h.astype(BF16), wpg_ref[...])) * e
    out_ref[...] = _rms(h, g_fin_ref[...]) if final_norm else h


def _post_call(x2, oa, ob, p2, consts, *, tm, ff_chunk, final_norm):
    T, D = x2.shape
    row = lambda w: pl.BlockSpec((tm, w), lambda i: (i, 0))
    return pl.pallas_call(
        functools.partial(_post_kernel, ff_chunk=ff_chunk, final_norm=final_norm),
        grid=(T // tm,),
        in_specs=[row(D), row(oa.shape[1]), row(ob.shape[1]), row(p2.shape[1])]
                 + [_const_spec(c.shape) for c in consts],
        out_specs=row(D),
        out_shape=jax.ShapeDtypeStruct((T, D), F32),
        compiler_params=pltpu.CompilerParams(dimension_semantics=("parallel",),
                                             vmem_limit_bytes=VMEM_LIMIT),
        name="post",
    )(x2, oa, ob, p2, *consts)


def _rotate_half_cols(w):
    half = w.shape[-1] // 2
    return jnp.concatenate([-w[..., half:], w[..., :half]], axis=-1)


def _layer_weights(w_in, w_q_b, w_kv_b):
    d = w_in.shape[0]
    o = 0
    cols = []
    for wd in (MLA_Q_RANK, MLA_KV_RANK, MLA_ROPE, SB_HEADS * SB_DIM, SB_HEADS * SB_DIM, SB_HEADS * SB_DIM, d, d):
        cols.append(w_in[:, o:o + wd])
        o += wd
    w_cq, w_ckv, w_kpe, w_qs, w_ks, w_vs, w_ga, w_gb = cols

    def rope_tile(w):
        return jnp.pad(w, ((0, 0), (MLA_NOPE, LANES - MLA_NOPE - MLA_ROPE)))

    wa = jnp.concatenate([w_cq, w_ckv, rope_tile(w_kpe), rope_tile(_rotate_half_cols(w_kpe))], axis=1)
    wsb = jnp.concatenate([w_qs, w_ks, w_vs], axis=1)

    qb = w_q_b.reshape(MLA_Q_RANK, MLA_HEADS, MLA_NOPE + MLA_ROPE)
    pad_q = ((0, 0), (0, 0), (0, LANES - MLA_NOPE - MLA_ROPE))
    wq = jnp.pad(qb, pad_q).reshape(MLA_Q_RANK, MLA_HEADS * LANES)
    qb_rot = jnp.concatenate([jnp.zeros_like(qb[..., :MLA_NOPE]), _rotate_half_cols(qb[..., MLA_NOPE:])], axis=-1)
    wqr = jnp.pad(qb_rot, pad_q).reshape(MLA_Q_RANK, MLA_HEADS * LANES)

    kvb = w_kv_b.reshape(MLA_KV_RANK, MLA_HEADS, MLA_NOPE + MLA_V)
    wk = jnp.pad(kvb[..., :MLA_NOPE], ((0, 0), (0, 0), (0, LANES - MLA_NOPE))).reshape(MLA_KV_RANK, MLA_HEADS * LANES)
    wv = kvb[..., MLA_NOPE:].reshape(MLA_KV_RANK, MLA_HEADS * MLA_V)
    bf = lambda t: t.astype(BF16)
    return tuple(map(bf, (wa, wsb, wq, wqr, wk, wv, w_ga, w_gb)))


def _rope_freq_row():
    inv_freq = 1.0 / (ROPE_THETA ** (jnp.arange(0, MLA_ROPE, 2, dtype=F32) / MLA_ROPE))
    both = jnp.concatenate([inv_freq, inv_freq])
    return jnp.pad(both, (MLA_NOPE, LANES - MLA_NOPE - MLA_ROPE)).reshape(1, LANES)


def _tiles(B, S):
    T = B * S
    tm_proj = min(512, T)
    tm_post = min(256, T)
    tq = min(256, S)
    return tm_proj, tm_post, tq, tq


def kernel(x, p, positions, g_mix, w_in, g_q_a, w_q_b, g_kv_a, w_kv_b, w_br_mla, w_br_sb, w_out, g_ffn,
           w_ffn_gate, w_ffn_up, w_ffn_down, w_ple_gate, w_ple_proj, g_ple, g_final):
    B, S, D = x.shape
    T = B * S
    depth = w_in.shape[0]
    tm_proj, tm_post, tq, tk = _tiles(B, S)
    d_ff = w_ffn_gate.shape[-1]
    ff_chunk = d_ff // 2 if (d_ff // 2) % LANES == 0 else d_ff
    row = lambda g: g.reshape(1, -1).astype(F32)
    bf = lambda t: t.astype(BF16)

    pos2 = positions.reshape(T, 1).astype(jnp.int32)
    freq = _rope_freq_row()
    h = x.reshape(T, D)
    for i in range(depth):
        wa, wsb, wq, wqr, wk, wv, w_ga, w_gb = _layer_weights(w_in[i], w_q_b[i], w_kv_b[i])
        qm, km, vm, qs, ks, vs = _proj_call(h, pos2, freq, row(g_mix[i]), wa, wsb, row(g_q_a[i]), wq, wqr,
                                            row(g_kv_a[i]), wk, wv, tm=tm_proj)
        shp = lambda t: t.reshape(B, S, t.shape[-1])
        o_a = _mla_call(shp(qm), shp(km), shp(vm), tq=tq, tk=tk).reshape(T, -1)
        o_b = _sb_call(shp(qs), shp(ks), shp(vs), tq=tq, tk=tk).reshape(T, -1)
        consts = (row(g_mix[i]), w_ga, w_gb, bf(w_br_mla[i]), bf(w_br_sb[i]), bf(w_out[i]), row(g_ffn[i]),
                  bf(w_ffn_gate[i]), bf(w_ffn_up[i]), bf(w_ffn_down[i]), bf(w_ple_gate[i]), bf(w_ple_proj[i]),
                  row(g_ple[i]), row(g_final))
        h = _post_call(h, o_a, o_b, p[i].reshape(T, -1), consts, tm=tm_post, ff_chunk=ff_chunk,
                       final_norm=(i == depth - 1))
    return h.reshape(B, S, D)
```

```python
import functools
import math

import jax
import jax.numpy as jnp
from jax import lax
from jax.experimental import pallas as pl
from jax.experimental.pallas import tpu as pltpu

EPS = 1e-6
MLA_HEADS = 8
MLA_NOPE = 64
MLA_ROPE = 32
MLA_V = 64
MLA_Q_RANK = 384
MLA_KV_RANK = 256
ROPE_THETA = 10000.0
MLA_SCALE = 1.0 / math.sqrt(MLA_NOPE + MLA_ROPE)
SB_HEADS = 8
SB_DIM = 64
SB_SCALE = 1.0 / math.sqrt(SB_DIM)
NEG_INF = -1e30
LOG2E = math.log2(math.e)
SOFTPLUS_CLAMP = 64.0

LANES = 128
HEAD_PAIR = 2
VMEM_LIMIT = 56 * 1024 * 1024

F32 = jnp.float32
BF16 = jnp.bfloat16


def _rms(x, g):
    return x * lax.rsqrt(jnp.mean(x * x, axis=-1, keepdims=True) + EPS) * g


def _dot(a, b):
    return jnp.dot(a, b, preferred_element_type=F32)


def _dot_nt(a, b):
    return lax.dot_general(a, b, (((1,), (1,)), ((), ())), preferred_element_type=F32)


def _proj_kernel(x_ref, pos_ref, freq_ref, g_mix_ref, wa_ref, wsb_ref, gq_ref, wq_ref, wqr_ref,
                 gkv_ref, wk_ref, wv_ref,
                 qm_ref, km_ref, vm_ref, qs_ref, ks_ref, vs_ref):
    n = _rms(x_ref[...], g_mix_ref[...]).astype(BF16)
    pa = _dot(n, wa_ref[...])
    c_q = pa[:, :MLA_Q_RANK]
    c_kv = pa[:, MLA_Q_RANK:MLA_Q_RANK + MLA_KV_RANK]
    kpe = pa[:, 640:768]
    kpe_rot = pa[:, 768:896]

    ang = pos_ref[...].astype(F32) * freq_ref[...]
    cos_t = jnp.cos(ang)
    sin_t = jnp.sin(ang)

    nq = _rms(c_q, gq_ref[...]).astype(BF16)
    q = _dot(nq, wq_ref[...])
    q_rot = _dot(nq, wqr_ref[...])
    nkv = _rms(c_kv, gkv_ref[...]).astype(BF16)
    kn = _dot(nkv, wk_ref[...])
    k_pe = kpe * cos_t + kpe_rot * sin_t
    for h in range(MLA_HEADS):
        sl = slice(h * LANES, (h + 1) * LANES)
        qm_ref[:, sl] = ((q[:, sl] * cos_t + q_rot[:, sl] * sin_t) * (MLA_SCALE * LOG2E)).astype(BF16)
        km_ref[:, sl] = (kn[:, sl] + k_pe).astype(BF16)
    vm_ref[...] = _dot(nkv, wv_ref[...]).astype(BF16)

    sb = _dot(n, wsb_ref[...])
    w = SB_HEADS * SB_DIM
    qs_ref[...] = (sb[:, :w] * (SB_SCALE * LOG2E)).astype(BF16)
    ks_ref[...] = sb[:, w:2 * w].astype(BF16)
    vs_ref[...] = sb[:, 2 * w:].astype(BF16)


def _const_spec(shape):
    return pl.BlockSpec(shape, lambda *_: (0,) * len(shape), pipeline_mode=pl.Buffered(1))


def _proj_call(x2, pos2, freq, g_mix, wa, wsb, gq, wq, wqr, gkv, wk, wv, *, tm):
    T, D = x2.shape
    row = lambda w: pl.BlockSpec((tm, w), lambda i: (i, 0))
    consts = (freq, g_mix, wa, wsb, gq, wq, wqr, gkv, wk, wv)
    out_w = (MLA_HEADS * LANES, MLA_HEADS * LANES, MLA_HEADS * MLA_V,
             SB_HEADS * SB_DIM, SB_HEADS * SB_DIM, SB_HEADS * SB_DIM)
    return pl.pallas_call(
        _proj_kernel,
        grid=(T // tm,),
        in_specs=[row(D), row(1)] + [_const_spec(c.shape) for c in consts],
        out_specs=[row(w) for w in out_w],
        out_shape=[jax.ShapeDtypeStruct((T, w), BF16) for w in out_w],
        compiler_params=pltpu.CompilerParams(dimension_semantics=("parallel",),
                                             vmem_limit_bytes=VMEM_LIMIT),
        name="proj",
    )(x2, pos2, *consts)


ONES_ROWS = 16


def _mla_kernel(q_ref, k_ref, vt_ref, o_ref, acc_ref, m_ref, qt_ref, *, t, hp):
    qi = pl.program_id(2)
    acc_ref[...] = jnp.zeros_like(acc_ref)
    m_ref[...] = jnp.full_like(m_ref, NEG_INF)
    for h in range(hp):
        qt_ref[h] = q_ref[:, h * LANES:(h + 1) * LANES].astype(F32).T.astype(BF16)
    key_l = lax.broadcasted_iota(jnp.int32, (t, t), 0)
    qry_l = lax.broadcasted_iota(jnp.int32, (t, t), 1)
    ones = jnp.ones((ONES_ROWS, t), BF16)

    def step(j, masked):
        k0 = pl.multiple_of(j * t, t)
        scores = []
        for h in range(hp):
            k_h = k_ref[pl.ds(k0, t), h * LANES:(h + 1) * LANES]
            scores.append(_dot(k_h, qt_ref[h]))
        for h in range(hp):
            s = scores[h]
            if masked:
                s = jnp.where(key_l <= qry_l, s, NEG_INF)
            m_prev = m_ref[h]
            m_new = jnp.maximum(m_prev, jnp.max(s, axis=0, keepdims=True))
            p = jnp.exp2(s - m_new)
            v_aug = jnp.concatenate([vt_ref[h * MLA_V:(h + 1) * MLA_V, pl.ds(k0, t)], ones], axis=0)
            acc_ref[h] = jnp.exp2(m_prev - m_new) * acc_ref[h] + _dot(v_aug, p.astype(BF16))
            m_ref[h] = m_new

    lax.fori_loop(0, qi, lambda j, c: (step(j, False), c)[1], 0)
    step(qi, True)

    for pair in range(hp // HEAD_PAIR):
        outs = []
        for h in (pair * HEAD_PAIR, pair * HEAD_PAIR + 1):
            a = acc_ref[h]
            outs.append(a[:MLA_V] / a[MLA_V:MLA_V + 1])
        o_t = jnp.concatenate(outs, axis=0)
        o_ref[:, pair * LANES:(pair + 1) * LANES] = o_t.T.astype(o_ref.dtype)


def _mla_call(qm, km, vmt, *, t, hp):
    B, S, _ = qm.shape
    return pl.pallas_call(
        functools.partial(_mla_kernel, t=t, hp=hp),
        grid=(B, MLA_HEADS // hp, S // t),
        in_specs=[pl.BlockSpec((None, t, hp * LANES), lambda b, g, qi: (b, qi, g)),
                  pl.BlockSpec((None, S, hp * LANES), lambda b, g, qi: (b, 0, g)),
                  pl.BlockSpec((None, hp * MLA_V, S), lambda b, g, qi: (b, g, 0))],
        out_specs=pl.BlockSpec((None, t, hp * MLA_V), lambda b, g, qi: (b, qi, g)),
        out_shape=jax.ShapeDtypeStruct((B, S, MLA_HEADS * MLA_V), BF16),
        scratch_shapes=[pltpu.VMEM((hp, MLA_V + ONES_ROWS, t), F32),
                        pltpu.VMEM((hp, 1, t), F32),
                        pltpu.VMEM((hp, LANES, t), BF16)],
        compiler_params=pltpu.CompilerParams(dimension_semantics=("parallel", "parallel", "arbitrary"),
                                             vmem_limit_bytes=VMEM_LIMIT),
        name="mla_attn",
    )(qm, km, vmt)


def _sb_kernel(q_ref, k_ref, vt_ref, tri_ref, o_ref, acc_ref, carry_ref, qt_ref, *, t, hp):
    qi = pl.program_id(2)
    acc_ref[...] = jnp.zeros_like(acc_ref)
    carry_ref[...] = jnp.zeros_like(carry_ref)
    q_lane = lax.broadcasted_iota(jnp.int32, (t, LANES), 1)
    for pair in range(hp // HEAD_PAIR):
        q = q_ref[:, pair * LANES:(pair + 1) * LANES].astype(F32)
        qt_ref[pair * HEAD_PAIR] = jnp.where(q_lane < SB_DIM, q, 0.0).T.astype(BF16)
        qt_ref[pair * HEAD_PAIR + 1] = jnp.where(q_lane >= SB_DIM, q, 0.0).T.astype(BF16)
    key_l = lax.broadcasted_iota(jnp.int32, (t, t), 0)
    qry_l = lax.broadcasted_iota(jnp.int32, (t, t), 1)

    def step(j, masked):
        k0 = pl.multiple_of(j * t, t)
        valid = key_l < qry_l
        zs = []
        for h in range(hp):
            pair = h // HEAD_PAIR
            k_blk = k_ref[pl.ds(k0, t), pair * LANES:(pair + 1) * LANES]
            zs.append(_dot(k_blk, qt_ref[h]))
        cs = []
        for h in range(hp):
            z = zs[h]
            sp = jnp.maximum(z, jnp.log2(1.0 + jnp.exp2(jnp.minimum(z, SOFTPLUS_CLAMP))))
            if masked:
                sp = jnp.where(valid, sp, 0.0)
            hi = sp.astype(BF16)
            lo = (sp - hi.astype(F32)).astype(BF16)
            cs.append(_dot(tri_ref[...], jnp.concatenate([hi, lo], axis=0)))
        for h in range(hp):
            c = cs[h]
            carry = carry_ref[h]
            a = jnp.exp2(zs[h] - c)
            if masked:
                a = jnp.where(valid, a, 0.0)
            vt = vt_ref[h * SB_DIM:(h + 1) * SB_DIM, pl.ds(k0, t)]
            acc_ref[h] += _dot(vt, a.astype(BF16)) * jnp.exp2(carry)
            carry_ref[h] = carry - c[0:1, :]

    step(qi, True)
    lax.fori_loop(0, qi, lambda i, c: (step(qi - 1 - i, False), c)[1], 0)

    for pair in range(hp // HEAD_PAIR):
        o_t = jnp.concatenate([acc_ref[pair * HEAD_PAIR], acc_ref[pair * HEAD_PAIR + 1]], axis=0)
        o_ref[:, pair * LANES:(pair + 1) * LANES] = o_t.T.astype(o_ref.dtype)


def _sb_call(qs, ks, vst, *, t, hp):
    B, S, _ = qs.shape
    r = lax.broadcasted_iota(jnp.int32, (t, t), 0)
    c = lax.broadcasted_iota(jnp.int32, (t, t), 1)
    tri = (c >= r).astype(BF16)
    tri = jnp.concatenate([tri, tri], axis=1)
    w = hp * SB_DIM
    return pl.pallas_call(
        functools.partial(_sb_kernel, t=t, hp=hp),
        grid=(B, SB_HEADS // hp, S // t),
        in_specs=[pl.BlockSpec((None, t, w), lambda b, g, qi: (b, qi, g)),
                  pl.BlockSpec((None, S, w), lambda b, g, qi: (b, 0, g)),
                  pl.BlockSpec((None, w, S), lambda b, g, qi: (b, g, 0)),
                  pl.BlockSpec((t, 2 * t), lambda b, g, qi: (0, 0))],
        out_specs=pl.BlockSpec((None, t, w), lambda b, g, qi: (b, qi, g)),
        out_shape=jax.ShapeDtypeStruct((B, S, SB_HEADS * SB_DIM), BF16),
        scratch_shapes=[pltpu.VMEM((hp, SB_DIM, t), F32),
                        pltpu.VMEM((hp, 1, t), F32),
                        pltpu.VMEM((hp, LANES, t), BF16)],
        compiler_params=pltpu.CompilerParams(dimension_semantics=("parallel", "parallel", "arbitrary"),
                                             vmem_limit_bytes=VMEM_LIMIT),
        name="sb_attn",
    )(qs, ks, vst, tri)


def _post_kernel(x_ref, oa_ref, ob_ref, p_ref, g_mix_ref, wga_ref, wgb_ref, wbra_ref, wbrb_ref, wout_ref,
                 g_ffn_ref, wfg_ref, wfu_ref, wfd_ref, wpg_ref, wpp_ref, g_ple_ref, g_fin_ref, out_ref,
                 *, ff_chunk, final_norm):
    x = x_ref[...]
    n = _rms(x, g_mix_ref[...]).astype(BF16)
    merged = (jax.nn.sigmoid(_dot(n, wga_ref[...])) * _dot(oa_ref[...], wbra_ref[...])
              + jax.nn.sigmoid(_dot(n, wgb_ref[...])) * _dot(ob_ref[...], wbrb_ref[...]))
    h = x + _dot(merged.astype(BF16), wout_ref[...])

    n2 = _rms(h, g_ffn_ref[...]).astype(BF16)
    d_ff = wfg_ref.shape[1]
    ff = None
    for c0 in range(0, d_ff, ff_chunk):
        g = _dot(n2, wfg_ref[:, c0:c0 + ff_chunk])
        u = _dot(n2, wfu_ref[:, c0:c0 + ff_chunk])
        part = _dot((g * jax.nn.sigmoid(g) * u).astype(BF16), wfd_ref[c0:c0 + ff_chunk, :])
        ff = part if ff is None else ff + part
    h = h + ff

    e = _rms(_dot(p_ref[...].astype(BF16), wpp_ref[...]), g_ple_ref[...])
    h = h + jax.nn.sigmoid(_dot(h.astype(BF16), wpg_ref[...])) * e
    out_ref[...] = _rms(h, g_fin_ref[...]) if final_norm else h


def _post_call(x2, oa, ob, p2, consts, *, tm, ff_chunk, final_norm):
    T, D = x2.shape
    row = lambda w: pl.BlockSpec((tm, w), lambda i: (i, 0))
    return pl.pallas_call(
        functools.partial(_post_kernel, ff_chunk=ff_chunk, final_norm=final_norm),
        grid=(T // tm,),
        in_specs=[row(D), row(oa.shape[1]), row(ob.shape[1]), row(p2.shape[1])]
                 + [_const_spec(c.shape) for c in consts],
        out_specs=row(D),
        out_shape=jax.ShapeDtypeStruct((T, D), F32),
        compiler_params=pltpu.CompilerParams(dimension_semantics=("parallel",),
                                             vmem_limit_bytes=VMEM_LIMIT),
        name="post",
    )(x2, oa, ob, p2, *consts)


def _rotate_half_cols(w):
    half = w.shape[-1] // 2
    return jnp.concatenate([-w[..., half:], w[..., :half]], axis=-1)


def _layer_weights(w_in, w_q_b, w_kv_b):
    d = w_in.shape[0]
    o = 0
    cols = []
    for wd in (MLA_Q_RANK, MLA_KV_RANK, MLA_ROPE, SB_HEADS * SB_DIM, SB_HEADS * SB_DIM, SB_HEADS * SB_DIM, d, d):
        cols.append(w_in[:, o:o + wd])
        o += wd
    w_cq, w_ckv, w_kpe, w_qs, w_ks, w_vs, w_ga, w_gb = cols

    def rope_tile(w):
        return jnp.pad(w, ((0, 0), (MLA_NOPE, LANES - MLA_NOPE - MLA_ROPE)))

    wa = jnp.concatenate([w_cq, w_ckv, rope_tile(w_kpe), rope_tile(_rotate_half_cols(w_kpe))], axis=1)
    wsb = jnp.concatenate([w_qs, w_ks, w_vs], axis=1)

    qb = w_q_b.reshape(MLA_Q_RANK, MLA_HEADS, MLA_NOPE + MLA_ROPE)
    pad_q = ((0, 0), (0, 0), (0, LANES - MLA_NOPE - MLA_ROPE))
    wq = jnp.pad(qb, pad_q).reshape(MLA_Q_RANK, MLA_HEADS * LANES)
    qb_rot = jnp.concatenate([jnp.zeros_like(qb[..., :MLA_NOPE]), _rotate_half_cols(qb[..., MLA_NOPE:])], axis=-1)
    wqr = jnp.pad(qb_rot, pad_q).reshape(MLA_Q_RANK, MLA_HEADS * LANES)

    kvb = w_kv_b.reshape(MLA_KV_RANK, MLA_HEADS, MLA_NOPE + MLA_V)
    wk = jnp.pad(kvb[..., :MLA_NOPE], ((0, 0), (0, 0), (0, LANES - MLA_NOPE))).reshape(MLA_KV_RANK, MLA_HEADS * LANES)
    wv = kvb[..., MLA_NOPE:].reshape(MLA_KV_RANK, MLA_HEADS * MLA_V)
    bf = lambda t: t.astype(BF16)
    return tuple(map(bf, (wa, wsb, wq, wqr, wk, wv, w_ga, w_gb)))


def _rope_freq_row():
    inv_freq = 1.0 / (ROPE_THETA ** (jnp.arange(0, MLA_ROPE, 2, dtype=F32) / MLA_ROPE))
    both = jnp.concatenate([inv_freq, inv_freq])
    return jnp.pad(both, (MLA_NOPE, LANES - MLA_NOPE - MLA_ROPE)).reshape(1, LANES)


def _tiles(B, S):
    T = B * S
    tm_proj = min(512, T)
    tm_post = min(256, T)
    t_attn = min(256, S)
    heads_per_step = 8
    return tm_proj, tm_post, t_attn, heads_per_step


def kernel(x, p, positions, g_mix, w_in, g_q_a, w_q_b, g_kv_a, w_kv_b, w_br_mla, w_br_sb, w_out, g_ffn,
           w_ffn_gate, w_ffn_up, w_ffn_down, w_ple_gate, w_ple_proj, g_ple, g_final):
    B, S, D = x.shape
    T = B * S
    depth = w_in.shape[0]
    tm_proj, tm_post, t_attn, hp = _tiles(B, S)
    d_ff = w_ffn_gate.shape[-1]
    ff_chunk = d_ff // 2 if (d_ff // 2) % LANES == 0 else d_ff
    row = lambda g: g.reshape(1, -1).astype(F32)
    bf = lambda t: t.astype(BF16)

    pos2 = positions.reshape(T, 1).astype(jnp.int32)
    freq = _rope_freq_row()
    h = x.reshape(T, D)
    for i in range(depth):
        wa, wsb, wq, wqr, wk, wv, w_ga, w_gb = _layer_weights(w_in[i], w_q_b[i], w_kv_b[i])
        qm, km, vm, qs, ks, vs = _proj_call(h, pos2, freq, row(g_mix[i]), wa, wsb, row(g_q_a[i]), wq, wqr,
                                            row(g_kv_a[i]), wk, wv, tm=tm_proj)
        shp = lambda t: t.reshape(B, S, t.shape[-1])
        shp_t = lambda t: jnp.swapaxes(shp(t), 1, 2)
        o_a = _mla_call(shp(qm), shp(km), shp_t(vm), t=t_attn, hp=hp).reshape(T, -1)
        o_b = _sb_call(shp(qs), shp(ks), shp_t(vs), t=t_attn, hp=hp).reshape(T, -1)
        consts = (row(g_mix[i]), w_ga, w_gb, bf(w_br_mla[i]), bf(w_br_sb[i]), bf(w_out[i]), row(g_ffn[i]),
                  bf(w_ffn_gate[i]), bf(w_ffn_up[i]), bf(w_ffn_down[i]), bf(w_ple_gate[i]), bf(w_ple_proj[i]),
                  row(g_ple[i]), row(g_final))
        h = _post_call(h, o_a, o_b, p[i].reshape(T, -1), consts, tm=tm_post, ff_chunk=ff_chunk,
                       final_norm=(i == depth - 1))
    return h.reshape(B, S, D)
```

```python
import functools
import math

import jax
import jax.numpy as jnp
from jax import lax
from jax.experimental import pallas as pl
from jax.experimental.pallas import tpu as pltpu

EPS = 1e-6
MLA_HEADS = 8
MLA_NOPE = 64
MLA_ROPE = 32
MLA_V = 64
MLA_Q_RANK = 384
MLA_KV_RANK = 256
ROPE_THETA = 10000.0
MLA_SCALE = 1.0 / math.sqrt(MLA_NOPE + MLA_ROPE)
SB_HEADS = 8
SB_DIM = 64
SB_SCALE = 1.0 / math.sqrt(SB_DIM)
NEG_INF = -1e30
LOG2E = math.log2(math.e)
SOFTPLUS_CLAMP = 64.0

LANES = 128
HEAD_PAIR = 2
VMEM_LIMIT = 56 * 1024 * 1024

F32 = jnp.float32
BF16 = jnp.bfloat16


def _rms(x, g):
    return x * lax.rsqrt(jnp.mean(x * x, axis=-1, keepdims=True) + EPS) * g


def _dot(a, b):
    return jnp.dot(a, b, preferred_element_type=F32)


def _dot_nt(a, b):
    return lax.dot_general(a, b, (((1,), (1,)), ((), ())), preferred_element_type=F32)


def _proj_kernel(x_ref, pos_ref, freq_ref, g_mix_ref, wa_ref, wsb_ref, gq_ref, wq_ref, wqr_ref,
                 gkv_ref, wk_ref, wv_ref,
                 qm_ref, km_ref, vm_ref, qs_ref, ks_ref, vs_ref):
    n = _rms(x_ref[...], g_mix_ref[...]).astype(BF16)
    pa = _dot(n, wa_ref[...])
    c_q = pa[:, :MLA_Q_RANK]
    c_kv = pa[:, MLA_Q_RANK:MLA_Q_RANK + MLA_KV_RANK]
    kpe = pa[:, 640:768]
    kpe_rot = pa[:, 768:896]

    ang = pos_ref[...].astype(F32) * freq_ref[...]
    cos_t = jnp.cos(ang)
    sin_t = jnp.sin(ang)

    nq = _rms(c_q, gq_ref[...]).astype(BF16)
    q = _dot(nq, wq_ref[...])
    q_rot = _dot(nq, wqr_ref[...])
    nkv = _rms(c_kv, gkv_ref[...]).astype(BF16)
    kn = _dot(nkv, wk_ref[...])
    k_pe = kpe * cos_t + kpe_rot * sin_t
    for h in range(MLA_HEADS):
        sl = slice(h * LANES, (h + 1) * LANES)
        qm_ref[:, sl] = ((q[:, sl] * cos_t + q_rot[:, sl] * sin_t) * (MLA_SCALE * LOG2E)).astype(BF16)
        km_ref[:, sl] = (kn[:, sl] + k_pe).astype(BF16)
    vm_ref[...] = _dot(nkv, wv_ref[...]).astype(BF16).T

    sb = _dot(n, wsb_ref[...])
    w = SB_HEADS * SB_DIM
    qs_ref[...] = (sb[:, :w] * (SB_SCALE * LOG2E)).astype(BF16)
    ks_ref[...] = sb[:, w:2 * w].astype(BF16)
    vs_ref[...] = sb[:, 2 * w:].astype(BF16).T


def _const_spec(shape):
    return pl.BlockSpec(shape, lambda *_: (0,) * len(shape), pipeline_mode=pl.Buffered(1))


def _proj_call(x2, pos2, freq, g_mix, wa, wsb, gq, wq, wqr, gkv, wk, wv, *, tm):
    T, D = x2.shape
    row = lambda w: pl.BlockSpec((tm, w), lambda i: (i, 0))
    consts = (freq, g_mix, wa, wsb, gq, wq, wqr, gkv, wk, wv)
    col = lambda w: pl.BlockSpec((w, tm), lambda i: (0, i))
    w_mla, w_v, w_sb = MLA_HEADS * LANES, MLA_HEADS * MLA_V, SB_HEADS * SB_DIM
    tok = lambda w: jax.ShapeDtypeStruct((T, w), BF16)
    tok_t = lambda w: jax.ShapeDtypeStruct((w, T), BF16)
    return pl.pallas_call(
        _proj_kernel,
        grid=(T // tm,),
        in_specs=[row(D), row(1)] + [_const_spec(c.shape) for c in consts],
        out_specs=[row(w_mla), row(w_mla), col(w_v), row(w_sb), row(w_sb), col(w_sb)],
        out_shape=[tok(w_mla), tok(w_mla), tok_t(w_v), tok(w_sb), tok(w_sb), tok_t(w_sb)],
        compiler_params=pltpu.CompilerParams(dimension_semantics=("parallel",),
                                             vmem_limit_bytes=VMEM_LIMIT),
        name="proj",
    )(x2, pos2, *consts)


ONES_ROWS = 16


def _mla_kernel(q_ref, k_ref, vt_ref, o_ref, acc_ref, m_ref, qt_ref, p_ref, alpha_ref, *, t, hp, wide):
    qi = pl.program_id(2)
    acc_ref[...] = jnp.zeros_like(acc_ref)
    m_ref[...] = jnp.full_like(m_ref, NEG_INF)
    for h in range(hp):
        qt_ref[h] = q_ref[:, h * LANES:(h + 1) * LANES].astype(F32).T.astype(BF16)
    key_l = lax.broadcasted_iota(jnp.int32, (t, t), 0)
    qry_l = lax.broadcasted_iota(jnp.int32, (t, t), 1)

    def scores(j, n_blk):
        k0 = pl.multiple_of(j * t, t)
        return [_dot(k_ref[pl.ds(k0, n_blk * t), h * LANES:(h + 1) * LANES], qt_ref[h]) for h in range(hp)]

    def softmax(h, s, masked):
        if masked:
            s = jnp.where(key_l <= qry_l, s, NEG_INF)
        m_prev = m_ref[h]
        m_new = jnp.maximum(m_prev, jnp.max(s, axis=0, keepdims=True))
        m_ref[h] = m_new
        return jnp.exp2(s - m_new).astype(BF16), jnp.exp2(m_prev - m_new)

    def pv(h, j, n_blk, p, alpha):
        k0 = pl.multiple_of(j * t, t)
        w = n_blk * t
        v_aug = jnp.concatenate([vt_ref[h * MLA_V:(h + 1) * MLA_V, pl.ds(k0, w)],
                                 jnp.ones((ONES_ROWS, w), BF16)], axis=0)
        acc_ref[h] = alpha * acc_ref[h] + _dot(v_aug, p)

    def step(j, n_blk, masked):
        s_all = scores(j, n_blk)
        for h in range(hp):
            p, alpha = softmax(h, s_all[h], masked)
            pv(h, j, n_blk, p, alpha)

    def refill(g, slot, s_all):
        for h in range(hp):
            p_ref[slot, h], alpha_ref[slot, h] = softmax(h, s_all[h], False)

    def back(g, slot):
        for h in range(hp):
            pv(h, g * wide, wide, p_ref[slot, h], alpha_ref[slot, h])

    def pipelined(g, slot):
        s_all = scores(g * wide, wide)
        back(g - 1, 1 - slot)
        refill(g, slot, s_all)

    def body(i, c):
        pipelined(2 * i + 1, 1)
        pipelined(2 * i + 2, 0)
        return c

    n_wide = qi // wide
    n_pairs = jnp.maximum(n_wide - 1, 0) // 2

    @pl.when(n_wide >= 1)
    def _():
        refill(0, 0, scores(0, wide))

    lax.fori_loop(0, n_pairs, body, 0)

    last_is_odd = jnp.logical_and(n_wide >= 2, n_wide % 2 == 0)

    @pl.when(last_is_odd)
    def _():
        pipelined(n_wide - 1, 1)

    @pl.when(last_is_odd)
    def _():
        back(n_wide - 1, 1)

    @pl.when(n_wide % 2 == 1)
    def _():
        back(n_wide - 1, 0)

    for r in range(wide - 1):
        @pl.when(qi - n_wide * wide > r)
        def _():
            step(n_wide * wide + r, 1, False)
    step(qi, 1, True)

    for pair in range(hp // HEAD_PAIR):
        outs = []
        for h in (pair * HEAD_PAIR, pair * HEAD_PAIR + 1):
            a = acc_ref[h]
            outs.append(a[:MLA_V] / a[MLA_V:MLA_V + 1])
        o_t = jnp.concatenate(outs, axis=0)
        o_ref[:, pair * LANES:(pair + 1) * LANES] = o_t.T.astype(o_ref.dtype)


def _mla_call(qm, km, vmt, *, t, hp, wide):
    B, S, _ = qm.shape
    return pl.pallas_call(
        functools.partial(_mla_kernel, t=t, hp=hp, wide=wide),
        grid=(B, MLA_HEADS // hp, S // t),
        in_specs=[pl.BlockSpec((None, t, hp * LANES), lambda b, g, qi: (b, qi, g)),
                  pl.BlockSpec((None, S, hp * LANES), lambda b, g, qi: (b, 0, g)),
                  pl.BlockSpec((hp * MLA_V, S), lambda b, g, qi: (g, b))],
        out_specs=pl.BlockSpec((None, t, hp * MLA_V), lambda b, g, qi: (b, qi, g)),
        out_shape=jax.ShapeDtypeStruct((B, S, MLA_HEADS * MLA_V), BF16),
        scratch_shapes=[pltpu.VMEM((hp, MLA_V + ONES_ROWS, t), F32),
                        pltpu.VMEM((hp, 1, t), F32),
                        pltpu.VMEM((hp, LANES, t), BF16),
                        pltpu.VMEM((2, hp, wide * t, t), BF16),
                        pltpu.VMEM((2, hp, 1, t), F32)],
        compiler_params=pltpu.CompilerParams(dimension_semantics=("parallel", "parallel", "arbitrary"),
                                             vmem_limit_bytes=VMEM_LIMIT),
        name="mla_attn",
    )(qm, km, vmt)


def _sb_kernel(q_ref, k_ref, vt_ref, tri_ref, o_ref, acc_ref, carry_ref, qt_ref, a_ref, tot_ref, *, t, hp, wide):
    qi = pl.program_id(2)
    acc_ref[...] = jnp.zeros_like(acc_ref)
    carry_ref[...] = jnp.zeros_like(carry_ref)
    q_lane = lax.broadcasted_iota(jnp.int32, (t, LANES), 1)
    for pair in range(hp // HEAD_PAIR):
        q = q_ref[:, pair * LANES:(pair + 1) * LANES].astype(F32)
        qt_ref[pair * HEAD_PAIR] = jnp.where(q_lane < SB_DIM, q, 0.0).T.astype(BF16)
        qt_ref[pair * HEAD_PAIR + 1] = jnp.where(q_lane >= SB_DIM, q, 0.0).T.astype(BF16)
    key_l = lax.broadcasted_iota(jnp.int32, (t, t), 0)
    qry_l = lax.broadcasted_iota(jnp.int32, (t, t), 1)

    valid = key_l < qry_l

    def key_offsets(j_top, n_blk):
        return [pl.multiple_of((j_top - b) * t, t) for b in range(n_blk)]

    def scores(j_top, n_blk):
        zs = {}
        for b, k0 in enumerate(key_offsets(j_top, n_blk)):
            for h in range(hp):
                pair = h // HEAD_PAIR
                k_blk = k_ref[pl.ds(k0, t), pair * LANES:(pair + 1) * LANES]
                zs[b, h] = _dot(k_blk, qt_ref[h])
        return zs

    def suffix(z, masked):
        sp = jnp.maximum(z, jnp.log2(1.0 + jnp.exp2(jnp.minimum(z, SOFTPLUS_CLAMP))))
        if masked:
            sp = jnp.where(valid, sp, 0.0)
        c = _dot(tri_ref[...], sp.astype(BF16))
        a = jnp.exp2(z - c)
        if masked:
            a = jnp.where(valid, a, 0.0)
        return a.astype(BF16), c[0:1, :]

    def weights_pv(h, k0, a, total):
        vt = vt_ref[h * SB_DIM:(h + 1) * SB_DIM, pl.ds(k0, t)]
        carry = carry_ref[h]
        acc_ref[h] += _dot(vt, a) * jnp.exp2(carry)
        carry_ref[h] = carry - total

    def step(j_top, n_blk, masked):
        zs = scores(j_top, n_blk)
        mid = {bh: suffix(z, masked) for bh, z in zs.items()}
        for b, k0 in enumerate(key_offsets(j_top, n_blk)):
            for h in range(hp):
                weights_pv(h, k0, *mid[b, h])

    def group_top(g):
        return qi - 1 - g * wide

    def refill(g, zs):
        for (b, h), z in zs.items():
            a_ref[g % 2, b, h], tot_ref[g % 2, b, h] = suffix(z, False)

    def back(g):
        for b, k0 in enumerate(key_offsets(group_top(g), wide)):
            for h in range(hp):
                weights_pv(h, k0, a_ref[g % 2, b, h], tot_ref[g % 2, b, h])

    def body(g, c):
        zs = scores(group_top(g), wide)
        back(g - 1)
        refill(g, zs)
        return c

    step(qi, 1, True)
    n_wide = qi // wide

    @pl.when(n_wide >= 1)
    def _():
        refill(0, scores(group_top(0), wide))

    lax.fori_loop(1, n_wide, body, 0)

    @pl.when(n_wide >= 1)
    def _():
        back(n_wide - 1)

    for r in range(wide - 1):
        @pl.when(qi - n_wide * wide > r)
        def _():
            step(qi - 1 - n_wide * wide - r, 1, False)

    for pair in range(hp // HEAD_PAIR):
        o_t = jnp.concatenate([acc_ref[pair * HEAD_PAIR], acc_ref[pair * HEAD_PAIR + 1]], axis=0)
        o_ref[:, pair * LANES:(pair + 1) * LANES] = o_t.T.astype(o_ref.dtype)


def _sb_call(qs, ks, vst, *, t, hp, wide):
    B, S, _ = qs.shape
    r = lax.broadcasted_iota(jnp.int32, (t, t), 0)
    c = lax.broadcasted_iota(jnp.int32, (t, t), 1)
    tri = (c >= r).astype(BF16)
    w = hp * SB_DIM
    return pl.pallas_call(
        functools.partial(_sb_kernel, t=t, hp=hp, wide=wide),
        grid=(B, SB_HEADS // hp, S // t),
        in_specs=[pl.BlockSpec((None, t, w), lambda b, g, qi: (b, qi, g)),
                  pl.BlockSpec((None, S, w), lambda b, g, qi: (b, 0, g)),
                  pl.BlockSpec((w, S), lambda b, g, qi: (g, b)),
                  pl.BlockSpec((t, t), lambda b, g, qi: (0, 0))],
        out_specs=pl.BlockSpec((None, t, w), lambda b, g, qi: (b, qi, g)),
        out_shape=jax.ShapeDtypeStruct((B, S, SB_HEADS * SB_DIM), BF16),
        scratch_shapes=[pltpu.VMEM((hp, SB_DIM, t), F32),
                        pltpu.VMEM((hp, 1, t), F32),
                        pltpu.VMEM((hp, LANES, t), BF16),
                        pltpu.VMEM((2, wide, hp, t, t), BF16),
                        pltpu.VMEM((2, wide, hp, 1, t), F32)],
        compiler_params=pltpu.CompilerParams(dimension_semantics=("parallel", "parallel", "arbitrary"),
                                             vmem_limit_bytes=VMEM_LIMIT),
        name="sb_attn",
    )(qs, ks, vst, tri)


def _post_kernel(x_ref, oa_ref, ob_ref, p_ref, g_mix_ref, wga_ref, wgb_ref, wbra_ref, wbrb_ref, wout_ref,
                 g_ffn_ref, wfg_ref, wfu_ref, wfd_ref, wpg_ref, wpp_ref, g_ple_ref, g_fin_ref, out_ref,
                 *, ff_chunk, final_norm):
    x = x_ref[...]
    n = _rms(x, g_mix_ref[...]).astype(BF16)
    merged = (jax.nn.sigmoid(_dot(n, wga_ref[...])) * _dot(oa_ref[...], wbra_ref[...])
              + jax.nn.sigmoid(_dot(n, wgb_ref[...])) * _dot(ob_ref[...], wbrb_ref[...]))
    h = x + _dot(merged.astype(BF16), wout_ref[...])

    n2 = _rms(h, g_ffn_ref[...]).astype(BF16)
    d_ff = wfg_ref.shape[1]
    ff = None
    for c0 in range(0, d_ff, ff_chunk):
        g = _dot(n2, wfg_ref[:, c0:c0 + ff_chunk])
        u = _dot(n2, wfu_ref[:, c0:c0 + ff_chunk])
        part = _dot((g * jax.nn.sigmoid(g) * u).astype(BF16), wfd_ref[c0:c0 + ff_chunk, :])
        ff = part if ff is None else ff + part
    h = h + ff

    e = _rms(_dot(p_ref[...].astype(BF16), wpp_ref[...]), g_ple_ref[...])
    h = h + jax.nn.sigmoid(_dot(h.astype(BF16), wpg_ref[...])) * e
    out_ref[...] = _rms(h, g_fin_ref[...]) if final_norm else h


def _post_call(x2, oa, ob, p2, consts, *, tm, ff_chunk, final_norm):
    T, D = x2.shape
    row = lambda w: pl.BlockSpec((tm, w), lambda i: (i, 0))
    return pl.pallas_call(
        functools.partial(_post_kernel, ff_chunk=ff_chunk, final_norm=final_norm),
        grid=(T // tm,),
        in_specs=[row(D), row(oa.shape[1]), row(ob.shape[1]), row(p2.shape[1])]
                 + [_const_spec(c.shape) for c in consts],
        out_specs=row(D),
        out_shape=jax.ShapeDtypeStruct((T, D), F32),
        compiler_params=pltpu.CompilerParams(dimension_semantics=("parallel",),
                                             vmem_limit_bytes=VMEM_LIMIT),
        name="post",
    )(x2, oa, ob, p2, *consts)


def _rotate_half_cols(w):
    half = w.shape[-1] // 2
    return jnp.concatenate([-w[..., half:], w[..., :half]], axis=-1)


def _layer_weights(w_in, w_q_b, w_kv_b):
    d = w_in.shape[0]
    o = 0
    cols = []
    for wd in (MLA_Q_RANK, MLA_KV_RANK, MLA_ROPE, SB_HEADS * SB_DIM, SB_HEADS * SB_DIM, SB_HEADS * SB_DIM, d, d):
        cols.append(w_in[:, o:o + wd])
        o += wd
    w_cq, w_ckv, w_kpe, w_qs, w_ks, w_vs, w_ga, w_gb = cols

    def rope_tile(w):
        return jnp.pad(w, ((0, 0), (MLA_NOPE, LANES - MLA_NOPE - MLA_ROPE)))

    wa = jnp.concatenate([w_cq, w_ckv, rope_tile(w_kpe), rope_tile(_rotate_half_cols(w_kpe))], axis=1)
    wsb = jnp.concatenate([w_qs, w_ks, w_vs], axis=1)

    qb = w_q_b.reshape(MLA_Q_RANK, MLA_HEADS, MLA_NOPE + MLA_ROPE)
    pad_q = ((0, 0), (0, 0), (0, LANES - MLA_NOPE - MLA_ROPE))
    wq = jnp.pad(qb, pad_q).reshape(MLA_Q_RANK, MLA_HEADS * LANES)
    qb_rot = jnp.concatenate([jnp.zeros_like(qb[..., :MLA_NOPE]), _rotate_half_cols(qb[..., MLA_NOPE:])], axis=-1)
    wqr = jnp.pad(qb_rot, pad_q).reshape(MLA_Q_RANK, MLA_HEADS * LANES)

    kvb = w_kv_b.reshape(MLA_KV_RANK, MLA_HEADS, MLA_NOPE + MLA_V)
    wk = jnp.pad(kvb[..., :MLA_NOPE], ((0, 0), (0, 0), (0, LANES - MLA_NOPE))).reshape(MLA_KV_RANK, MLA_HEADS * LANES)
    wv = kvb[..., MLA_NOPE:].reshape(MLA_KV_RANK, MLA_HEADS * MLA_V)
    bf = lambda t: t.astype(BF16)
    return tuple(map(bf, (wa, wsb, wq, wqr, wk, wv, w_ga, w_gb)))


def _rope_freq_row():
    inv_freq = 1.0 / (ROPE_THETA ** (jnp.arange(0, MLA_ROPE, 2, dtype=F32) / MLA_ROPE))
    both = jnp.concatenate([inv_freq, inv_freq])
    return jnp.pad(both, (MLA_NOPE, LANES - MLA_NOPE - MLA_ROPE)).reshape(1, LANES)


def _tiles(B, S):
    T = B * S
    tm_proj = min(512, T)
    tm_post = min(256, T)
    t_attn = min(256, S)
    heads_per_step = 8
    wide = 2
    return tm_proj, tm_post, t_attn, heads_per_step, wide


def kernel(x, p, positions, g_mix, w_in, g_q_a, w_q_b, g_kv_a, w_kv_b, w_br_mla, w_br_sb, w_out, g_ffn,
           w_ffn_gate, w_ffn_up, w_ffn_down, w_ple_gate, w_ple_proj, g_ple, g_final):
    B, S, D = x.shape
    T = B * S
    depth = w_in.shape[0]
    tm_proj, tm_post, t_attn, hp, wide = _tiles(B, S)
    d_ff = w_ffn_gate.shape[-1]
    ff_chunk = d_ff // 2 if (d_ff // 2) % LANES == 0 else d_ff
    row = lambda g: g.reshape(1, -1).astype(F32)
    bf = lambda t: t.astype(BF16)

    pos2 = positions.reshape(T, 1).astype(jnp.int32)
    freq = _rope_freq_row()
    h = x.reshape(T, D)
    for i in range(depth):
        wa, wsb, wq, wqr, wk, wv, w_ga, w_gb = _layer_weights(w_in[i], w_q_b[i], w_kv_b[i])
        qm, km, vm, qs, ks, vs = _proj_call(h, pos2, freq, row(g_mix[i]), wa, wsb, row(g_q_a[i]), wq, wqr,
                                            row(g_kv_a[i]), wk, wv, tm=tm_proj)
        shp = lambda t: t.reshape(B, S, t.shape[-1])
        o_a = _mla_call(shp(qm), shp(km), vm, t=t_attn, hp=hp, wide=wide).reshape(T, -1)
        o_b = _sb_call(shp(qs), shp(ks), vs, t=t_attn, hp=hp, wide=wide).reshape(T, -1)
        consts = (row(g_mix[i]), w_ga, w_gb, bf(w_br_mla[i]), bf(w_br_sb[i]), bf(w_out[i]), row(g_ffn[i]),
                  bf(w_ffn_gate[i]), bf(w_ffn_up[i]), bf(w_ffn_down[i]), bf(w_ple_gate[i]), bf(w_ple_proj[i]),
                  row(g_ple[i]), row(g_final))
        h = _post_call(h, o_a, o_b, p[i].reshape(T, -1), consts, tm=tm_post, ff_chunk=ff_chunk,
                       final_norm=(i == depth - 1))
    return h.reshape(B, S, D)
```

```python
import functools
import math

import jax
import jax.numpy as jnp
from jax import lax
from jax.experimental import pallas as pl
from jax.experimental.pallas import tpu as pltpu

EPS = 1e-6
MLA_HEADS = 8
MLA_NOPE = 64
MLA_ROPE = 32
MLA_V = 64
MLA_Q_RANK = 384
MLA_KV_RANK = 256
ROPE_THETA = 10000.0
MLA_SCALE = 1.0 / math.sqrt(MLA_NOPE + MLA_ROPE)
SB_HEADS = 8
SB_DIM = 64
SB_SCALE = 1.0 / math.sqrt(SB_DIM)
NEG_INF = -1e30
LOG2E = math.log2(math.e)
SOFTPLUS_CLAMP = 64.0
EXP2_IS_ZERO_BELOW = -151.0

LANES = 128
HEAD_PAIR = 2
VMEM_LIMIT = 56 * 1024 * 1024

F32 = jnp.float32
BF16 = jnp.bfloat16


def _rms(x, g):
    return x * lax.rsqrt(jnp.mean(x * x, axis=-1, keepdims=True) + EPS) * g


def _dot(a, b):
    return jnp.dot(a, b, preferred_element_type=F32)


def _dot_nt(a, b):
    return lax.dot_general(a, b, (((1,), (1,)), ((), ())), preferred_element_type=F32)


def _proj_kernel(x_ref, pos_ref, freq_ref, g_mix_ref, wa_ref, wsb_ref, gq_ref, wq_ref, wqr_ref,
                 gkv_ref, wk_ref, wv_ref,
                 qm_ref, km_ref, vm_ref, qs_ref, ks_ref, vs_ref):
    n = _rms(x_ref[...], g_mix_ref[...]).astype(BF16)
    pa = _dot(n, wa_ref[...])
    c_q = pa[:, :MLA_Q_RANK]
    c_kv = pa[:, MLA_Q_RANK:MLA_Q_RANK + MLA_KV_RANK]
    kpe = pa[:, 640:768]
    kpe_rot = pa[:, 768:896]

    ang = pos_ref[...].astype(F32) * freq_ref[...]
    cos_t = jnp.cos(ang)
    sin_t = jnp.sin(ang)

    nq = _rms(c_q, gq_ref[...]).astype(BF16)
    q = _dot(nq, wq_ref[...])
    q_rot = _dot(nq, wqr_ref[...])
    nkv = _rms(c_kv, gkv_ref[...]).astype(BF16)
    kn = _dot(nkv, wk_ref[...])
    k_pe = kpe * cos_t + kpe_rot * sin_t
    for h in range(MLA_HEADS):
        sl = slice(h * LANES, (h + 1) * LANES)
        qm_ref[:, sl] = ((q[:, sl] * cos_t + q_rot[:, sl] * sin_t) * (MLA_SCALE * LOG2E)).astype(BF16)
        km_ref[:, sl] = (kn[:, sl] + k_pe).astype(BF16)
    vm_ref[...] = _dot(nkv, wv_ref[...]).astype(BF16).T

    sb = _dot(n, wsb_ref[...])
    w = SB_HEADS * SB_DIM
    qs_ref[...] = (sb[:, :w] * (SB_SCALE * LOG2E)).astype(BF16)
    ks_ref[...] = sb[:, w:2 * w].astype(BF16)
    vs_ref[...] = sb[:, 2 * w:].astype(BF16).T


def _const_spec(shape):
    return pl.BlockSpec(shape, lambda *_: (0,) * len(shape), pipeline_mode=pl.Buffered(1))


def _proj_call(x2, pos2, freq, g_mix, wa, wsb, gq, wq, wqr, gkv, wk, wv, *, tm):
    T, D = x2.shape
    row = lambda w: pl.BlockSpec((tm, w), lambda i: (i, 0))
    consts = (freq, g_mix, wa, wsb, gq, wq, wqr, gkv, wk, wv)
    col = lambda w: pl.BlockSpec((w, tm), lambda i: (0, i))
    w_mla, w_v, w_sb = MLA_HEADS * LANES, MLA_HEADS * MLA_V, SB_HEADS * SB_DIM
    tok = lambda w: jax.ShapeDtypeStruct((T, w), BF16)
    tok_t = lambda w: jax.ShapeDtypeStruct((w, T), BF16)
    return pl.pallas_call(
        _proj_kernel,
        grid=(T // tm,),
        in_specs=[row(D), row(1)] + [_const_spec(c.shape) for c in consts],
        out_specs=[row(w_mla), row(w_mla), col(w_v), row(w_sb), row(w_sb), col(w_sb)],
        out_shape=[tok(w_mla), tok(w_mla), tok_t(w_v), tok(w_sb), tok(w_sb), tok_t(w_sb)],
        compiler_params=pltpu.CompilerParams(dimension_semantics=("parallel",),
                                             vmem_limit_bytes=VMEM_LIMIT),
        name="proj",
    )(x2, pos2, *consts)


ONES_ROWS = 16


def _mla_kernel(q_ref, k_ref, vt_ref, o_ref, acc_ref, m_ref, qt_ref, p_ref, alpha_ref, *, t, hp, wide):
    qi = pl.program_id(2)
    acc_ref[...] = jnp.zeros_like(acc_ref)
    m_ref[...] = jnp.full_like(m_ref, NEG_INF)
    for h in range(hp):
        qt_ref[h] = q_ref[:, h * LANES:(h + 1) * LANES].astype(F32).T.astype(BF16)
    key_l = lax.broadcasted_iota(jnp.int32, (t, t), 0)
    qry_l = lax.broadcasted_iota(jnp.int32, (t, t), 1)

    def scores(j, n_blk):
        k0 = pl.multiple_of(j * t, t)
        return [_dot(k_ref[pl.ds(k0, n_blk * t), h * LANES:(h + 1) * LANES], qt_ref[h]) for h in range(hp)]

    def softmax(h, s, masked):
        if masked:
            s = jnp.where(key_l <= qry_l, s, NEG_INF)
        m_prev = m_ref[h]
        m_new = jnp.maximum(m_prev, jnp.max(s, axis=0, keepdims=True))
        m_ref[h] = m_new
        return jnp.exp2(s - m_new).astype(BF16), jnp.exp2(m_prev - m_new)

    def pv(h, j, n_blk, p, alpha):
        k0 = pl.multiple_of(j * t, t)
        w = n_blk * t
        v_aug = jnp.concatenate([vt_ref[h * MLA_V:(h + 1) * MLA_V, pl.ds(k0, w)],
                                 jnp.ones((ONES_ROWS, w), BF16)], axis=0)
        acc_ref[h] = alpha * acc_ref[h] + _dot(v_aug, p)

    def step(j, n_blk, masked):
        s_all = scores(j, n_blk)
        for h in range(hp):
            p, alpha = softmax(h, s_all[h], masked)
            pv(h, j, n_blk, p, alpha)

    def refill(g, slot, s_all):
        for h in range(hp):
            p_ref[slot, h], alpha_ref[slot, h] = softmax(h, s_all[h], False)

    def back(g, slot):
        for h in range(hp):
            pv(h, g * wide, wide, p_ref[slot, h], alpha_ref[slot, h])

    def pipelined(g, slot):
        s_all = scores(g * wide, wide)
        back(g - 1, 1 - slot)
        refill(g, slot, s_all)

    def body(i, c):
        pipelined(2 * i + 1, 1)
        pipelined(2 * i + 2, 0)
        return c

    n_wide = qi // wide
    n_pairs = jnp.maximum(n_wide - 1, 0) // 2

    @pl.when(n_wide >= 1)
    def _():
        refill(0, 0, scores(0, wide))

    lax.fori_loop(0, n_pairs, body, 0)

    last_is_odd = jnp.logical_and(n_wide >= 2, n_wide % 2 == 0)

    @pl.when(last_is_odd)
    def _():
        pipelined(n_wide - 1, 1)

    @pl.when(last_is_odd)
    def _():
        back(n_wide - 1, 1)

    @pl.when(n_wide % 2 == 1)
    def _():
        back(n_wide - 1, 0)

    for r in range(wide - 1):
        @pl.when(qi - n_wide * wide > r)
        def _():
            step(n_wide * wide + r, 1, False)
    step(qi, 1, True)

    for pair in range(hp // HEAD_PAIR):
        outs = []
        for h in (pair * HEAD_PAIR, pair * HEAD_PAIR + 1):
            a = acc_ref[h]
            outs.append(a[:MLA_V] / a[MLA_V:MLA_V + 1])
        o_t = jnp.concatenate(outs, axis=0)
        o_ref[:, pair * LANES:(pair + 1) * LANES] = o_t.T.astype(o_ref.dtype)


def _mla_call(qm, km, vmt, *, t, hp, wide):
    B, S, _ = qm.shape
    return pl.pallas_call(
        functools.partial(_mla_kernel, t=t, hp=hp, wide=wide),
        grid=(B, MLA_HEADS // hp, S // t),
        in_specs=[pl.BlockSpec((None, t, hp * LANES), lambda b, g, qi: (b, qi, g)),
                  pl.BlockSpec((None, S, hp * LANES), lambda b, g, qi: (b, 0, g)),
                  pl.BlockSpec((hp * MLA_V, S), lambda b, g, qi: (g, b))],
        out_specs=pl.BlockSpec((None, t, hp * MLA_V), lambda b, g, qi: (b, qi, g)),
        out_shape=jax.ShapeDtypeStruct((B, S, MLA_HEADS * MLA_V), BF16),
        scratch_shapes=[pltpu.VMEM((hp, MLA_V + ONES_ROWS, t), F32),
                        pltpu.VMEM((hp, 1, t), F32),
                        pltpu.VMEM((hp, LANES, t), BF16),
                        pltpu.VMEM((2, hp, wide * t, t), BF16),
                        pltpu.VMEM((2, hp, 1, t), F32)],
        compiler_params=pltpu.CompilerParams(dimension_semantics=("parallel", "parallel", "arbitrary"),
                                             vmem_limit_bytes=VMEM_LIMIT),
        name="mla_attn",
    )(qm, km, vmt)


def _sb_kernel(q_ref, k_ref, vt_ref, tri_ref, o_ref, acc_ref, carry_ref, qt_ref, a_ref, tot_ref, *, t, hp, wide):
    qi = pl.program_id(2)
    acc_ref[...] = jnp.zeros_like(acc_ref)
    carry_ref[...] = jnp.zeros_like(carry_ref)
    q_lane = lax.broadcasted_iota(jnp.int32, (t, LANES), 1)
    for pair in range(hp // HEAD_PAIR):
        q = q_ref[:, pair * LANES:(pair + 1) * LANES].astype(F32)
        qt_ref[pair * HEAD_PAIR] = jnp.where(q_lane < SB_DIM, q, 0.0).T.astype(BF16)
        qt_ref[pair * HEAD_PAIR + 1] = jnp.where(q_lane >= SB_DIM, q, 0.0).T.astype(BF16)
    key_l = lax.broadcasted_iota(jnp.int32, (t, t), 0)
    qry_l = lax.broadcasted_iota(jnp.int32, (t, t), 1)

    valid = key_l < qry_l

    def key_offsets(j_top, n_blk):
        return [pl.multiple_of((j_top - b) * t, t) for b in range(n_blk)]

    def scores(j_top, n_blk):
        zs = {}
        for b, k0 in enumerate(key_offsets(j_top, n_blk)):
            for h in range(hp):
                pair = h // HEAD_PAIR
                k_blk = k_ref[pl.ds(k0, t), pair * LANES:(pair + 1) * LANES]
                zs[b, h] = _dot(k_blk, qt_ref[h])
        return zs

    def suffix(z, masked):
        sp = jnp.maximum(z, jnp.log2(1.0 + jnp.exp2(jnp.minimum(z, SOFTPLUS_CLAMP))))
        if masked:
            sp = jnp.where(valid, sp, 0.0)
        c = _dot(tri_ref[...], sp.astype(BF16))
        a = jnp.exp2(z - c)
        if masked:
            a = jnp.where(valid, a, 0.0)
        return a.astype(BF16), c[0:1, :]

    def weights_pv(h, k0, a, total):
        vt = vt_ref[h * SB_DIM:(h + 1) * SB_DIM, pl.ds(k0, t)]
        carry = carry_ref[h]
        acc_ref[h] += _dot(vt, a) * jnp.exp2(carry)
        carry_ref[h] = carry - total

    def step(j_top, n_blk, masked):
        zs = scores(j_top, n_blk)
        mid = {bh: suffix(z, masked) for bh, z in zs.items()}
        for b, k0 in enumerate(key_offsets(j_top, n_blk)):
            for h in range(hp):
                weights_pv(h, k0, *mid[b, h])

    def group_top(g):
        return qi - 2 - g * wide

    def refill(g, zs):
        for (b, h), z in zs.items():
            a_ref[g % 2, b, h], tot_ref[g % 2, b, h] = suffix(z, False)

    def back(g):
        for b, k0 in enumerate(key_offsets(group_top(g), wide)):
            for h in range(hp):
                weights_pv(h, k0, a_ref[g % 2, b, h], tot_ref[g % 2, b, h])

    def alive():
        return (jnp.max(carry_ref[...]) > EXP2_IS_ZERO_BELOW).astype(jnp.int32)

    step(qi, 1, True)

    @pl.when(qi >= 1)
    def _():
        step(qi - 1, 1, False)

    n_rest = jnp.maximum(qi - 1, 0)
    n_wide = n_rest // wide

    @pl.when(jnp.logical_and(n_wide >= 1, alive() == 1))
    def _():
        refill(0, scores(group_top(0), wide))

    def cond(state):
        g, live = state
        return jnp.logical_and(g < n_wide, live == 1)

    def body(state):
        g, _ = state
        zs = scores(group_top(g), wide)
        back(g - 1)
        refill(g, zs)
        return g + 1, alive()

    g_end, live = lax.while_loop(cond, body, (jnp.int32(1), alive()))

    @pl.when(jnp.logical_and(n_wide >= 1, live == 1))
    def _():
        back(g_end - 1)

    for r in range(wide - 1):
        @pl.when(jnp.logical_and(n_rest - n_wide * wide > r, alive() == 1))
        def _():
            step(qi - 2 - n_wide * wide - r, 1, False)

    for pair in range(hp // HEAD_PAIR):
        o_t = jnp.concatenate([acc_ref[pair * HEAD_PAIR], acc_ref[pair * HEAD_PAIR + 1]], axis=0)
        o_ref[:, pair * LANES:(pair + 1) * LANES] = o_t.T.astype(o_ref.dtype)


def _sb_call(qs, ks, vst, *, t, hp, wide):
    B, S, _ = qs.shape
    r = lax.broadcasted_iota(jnp.int32, (t, t), 0)
    c = lax.broadcasted_iota(jnp.int32, (t, t), 1)
    tri = (c >= r).astype(BF16)
    w = hp * SB_DIM
    return pl.pallas_call(
        functools.partial(_sb_kernel, t=t, hp=hp, wide=wide),
        grid=(B, SB_HEADS // hp, S // t),
        in_specs=[pl.BlockSpec((None, t, w), lambda b, g, qi: (b, qi, g)),
                  pl.BlockSpec((None, S, w), lambda b, g, qi: (b, 0, g)),
                  pl.BlockSpec((w, S), lambda b, g, qi: (g, b)),
                  pl.BlockSpec((t, t), lambda b, g, qi: (0, 0))],
        out_specs=pl.BlockSpec((None, t, w), lambda b, g, qi: (b, qi, g)),
        out_shape=jax.ShapeDtypeStruct((B, S, SB_HEADS * SB_DIM), BF16),
        scratch_shapes=[pltpu.VMEM((hp, SB_DIM, t), F32),
                        pltpu.VMEM((hp, 1, t), F32),
                        pltpu.VMEM((hp, LANES, t), BF16),
                        pltpu.VMEM((2, wide, hp, t, t), BF16),
                        pltpu.VMEM((2, wide, hp, 1, t), F32)],
        compiler_params=pltpu.CompilerParams(dimension_semantics=("parallel", "parallel", "arbitrary"),
                                             vmem_limit_bytes=VMEM_LIMIT),
        name="sb_attn",
    )(qs, ks, vst, tri)


def _post_kernel(x_ref, oa_ref, ob_ref, p_ref, g_mix_ref, wga_ref, wgb_ref, wbra_ref, wbrb_ref, wout_ref,
                 g_ffn_ref, wfg_ref, wfu_ref, wfd_ref, wpg_ref, wpp_ref, g_ple_ref, g_fin_ref, out_ref,
                 *, ff_chunk, final_norm):
    x = x_ref[...]
    n = _rms(x, g_mix_ref[...]).astype(BF16)
    merged = (jax.nn.sigmoid(_dot(n, wga_ref[...])) * _dot(oa_ref[...], wbra_ref[...])
              + jax.nn.sigmoid(_dot(n, wgb_ref[...])) * _dot(ob_ref[...], wbrb_ref[...]))
    h = x + _dot(merged.astype(BF16), wout_ref[...])

    n2 = _rms(h, g_ffn_ref[...]).astype(BF16)
    d_ff = wfg_ref.shape[1]
    ff = None
    for c0 in range(0, d_ff, ff_chunk):
        g = _dot(n2, wfg_ref[:, c0:c0 + ff_chunk])
        u = _dot(n2, wfu_ref[:, c0:c0 + ff_chunk])
        part = _dot((g * jax.nn.sigmoid(g) * u).astype(BF16), wfd_ref[c0:c0 + ff_chunk, :])
        ff = part if ff is None else ff + part
    h = h + ff

    e = _rms(_dot(p_ref[...].astype(BF16), wpp_ref[...]), g_ple_ref[...])
    h = h + jax.nn.sigmoid(_dot(h.astype(BF16), wpg_ref[...])) * e
    out_ref[...] = _rms(h, g_fin_ref[...]) if final_norm else h


def _post_call(x2, oa, ob, p2, consts, *, tm, ff_chunk, final_norm):
    T, D = x2.shape
    row = lambda w: pl.BlockSpec((tm, w), lambda i: (i, 0))
    return pl.pallas_call(
        functools.partial(_post_kernel, ff_chunk=ff_chunk, final_norm=final_norm),
        grid=(T // tm,),
        in_specs=[row(D), row(oa.shape[1]), row(ob.shape[1]), row(p2.shape[1])]
                 + [_const_spec(c.shape) for c in consts],
        out_specs=row(D),
        out_shape=jax.ShapeDtypeStruct((T, D), F32),
        compiler_params=pltpu.CompilerParams(dimension_semantics=("parallel",),
                                             vmem_limit_bytes=VMEM_LIMIT),
        name="post",
    )(x2, oa, ob, p2, *consts)


def _rotate_half_cols(w):
    half = w.shape[-1] // 2
    return jnp.concatenate([-w[..., half:], w[..., :half]], axis=-1)


def _layer_weights(w_in, w_q_b, w_kv_b):
    d = w_in.shape[0]
    o = 0
    cols = []
    for wd in (MLA_Q_RANK, MLA_KV_RANK, MLA_ROPE, SB_HEADS * SB_DIM, SB_HEADS * SB_DIM, SB_HEADS * SB_DIM, d, d):
        cols.append(w_in[:, o:o + wd])
        o += wd
    w_cq, w_ckv, w_kpe, w_qs, w_ks, w_vs, w_ga, w_gb = cols

    def rope_tile(w):
        return jnp.pad(w, ((0, 0), (MLA_NOPE, LANES - MLA_NOPE - MLA_ROPE)))

    wa = jnp.concatenate([w_cq, w_ckv, rope_tile(w_kpe), rope_tile(_rotate_half_cols(w_kpe))], axis=1)
    wsb = jnp.concatenate([w_qs, w_ks, w_vs], axis=1)

    qb = w_q_b.reshape(MLA_Q_RANK, MLA_HEADS, MLA_NOPE + MLA_ROPE)
    pad_q = ((0, 0), (0, 0), (0, LANES - MLA_NOPE - MLA_ROPE))
    wq = jnp.pad(qb, pad_q).reshape(MLA_Q_RANK, MLA_HEADS * LANES)
    qb_rot = jnp.concatenate([jnp.zeros_like(qb[..., :MLA_NOPE]), _rotate_half_cols(qb[..., MLA_NOPE:])], axis=-1)
    wqr = jnp.pad(qb_rot, pad_q).reshape(MLA_Q_RANK, MLA_HEADS * LANES)

    kvb = w_kv_b.reshape(MLA_KV_RANK, MLA_HEADS, MLA_NOPE + MLA_V)
    wk = jnp.pad(kvb[..., :MLA_NOPE], ((0, 0), (0, 0), (0, LANES - MLA_NOPE))).reshape(MLA_KV_RANK, MLA_HEADS * LANES)
    wv = kvb[..., MLA_NOPE:].reshape(MLA_KV_RANK, MLA_HEADS * MLA_V)
    bf = lambda t: t.astype(BF16)
    return tuple(map(bf, (wa, wsb, wq, wqr, wk, wv, w_ga, w_gb)))


def _rope_freq_row():
    inv_freq = 1.0 / (ROPE_THETA ** (jnp.arange(0, MLA_ROPE, 2, dtype=F32) / MLA_ROPE))
    both = jnp.concatenate([inv_freq, inv_freq])
    return jnp.pad(both, (MLA_NOPE, LANES - MLA_NOPE - MLA_ROPE)).reshape(1, LANES)


def _tiles(B, S):
    T = B * S
    tm_proj = min(512, T)
    tm_post = min(256, T)
    t_attn = min(256, S)
    heads_per_step = 8
    wide = 2
    return tm_proj, tm_post, t_attn, heads_per_step, wide


def kernel(x, p, positions, g_mix, w_in, g_q_a, w_q_b, g_kv_a, w_kv_b, w_br_mla, w_br_sb, w_out, g_ffn,
           w_ffn_gate, w_ffn_up, w_ffn_down, w_ple_gate, w_ple_proj, g_ple, g_final):
    B, S, D = x.shape
    T = B * S
    depth = w_in.shape[0]
    tm_proj, tm_post, t_attn, hp, wide = _tiles(B, S)
    d_ff = w_ffn_gate.shape[-1]
    ff_chunk = d_ff // 2 if (d_ff // 2) % LANES == 0 else d_ff
    row = lambda g: g.reshape(1, -1).astype(F32)
    bf = lambda t: t.astype(BF16)

    pos2 = positions.reshape(T, 1).astype(jnp.int32)
    freq = _rope_freq_row()
    h = x.reshape(T, D)
    for i in range(depth):
        wa, wsb, wq, wqr, wk, wv, w_ga, w_gb = _layer_weights(w_in[i], w_q_b[i], w_kv_b[i])
        qm, km, vm, qs, ks, vs = _proj_call(h, pos2, freq, row(g_mix[i]), wa, wsb, row(g_q_a[i]), wq, wqr,
                                            row(g_kv_a[i]), wk, wv, tm=tm_proj)
        shp = lambda t: t.reshape(B, S, t.shape[-1])
        o_a = _mla_call(shp(qm), shp(km), vm, t=t_attn, hp=hp, wide=wide).reshape(T, -1)
        o_b = _sb_call(shp(qs), shp(ks), vs, t=t_attn, hp=hp, wide=wide).reshape(T, -1)
        consts = (row(g_mix[i]), w_ga, w_gb, bf(w_br_mla[i]), bf(w_br_sb[i]), bf(w_out[i]), row(g_ffn[i]),
                  bf(w_ffn_gate[i]), bf(w_ffn_up[i]), bf(w_ffn_down[i]), bf(w_ple_gate[i]), bf(w_ple_proj[i]),
                  row(g_ple[i]), row(g_final))
        h = _post_call(h, o_a, o_b, p[i].reshape(T, -1), consts, tm=tm_post, ff_chunk=ff_chunk,
                       final_norm=(i == depth - 1))
    return h.reshape(B, S, D)
```

```python
import functools
import math

import jax
import jax.numpy as jnp
from jax import lax
from jax.experimental import pallas as pl
from jax.experimental.pallas import tpu as pltpu

EPS = 1e-6
MLA_HEADS = 8
MLA_NOPE = 64
MLA_ROPE = 32
MLA_V = 64
MLA_Q_RANK = 384
MLA_KV_RANK = 256
ROPE_THETA = 10000.0
MLA_SCALE = 1.0 / math.sqrt(MLA_NOPE + MLA_ROPE)
SB_HEADS = 8
SB_DIM = 64
SB_SCALE = 1.0 / math.sqrt(SB_DIM)
NEG_INF = -1e30
LOG2E = math.log2(math.e)
SOFTPLUS_CLAMP = 64.0
EXP2_IS_ZERO_BELOW = -151.0

LANES = 128
HEAD_PAIR = 2
VMEM_LIMIT = 56 * 1024 * 1024

F32 = jnp.float32
BF16 = jnp.bfloat16


def _rms(x, g):
    return x * lax.rsqrt(jnp.mean(x * x, axis=-1, keepdims=True) + EPS) * g


def _dot(a, b):
    return jnp.dot(a, b, preferred_element_type=F32)


def _dot_nt(a, b):
    return lax.dot_general(a, b, (((1,), (1,)), ((), ())), preferred_element_type=F32)


def _proj_kernel(x_ref, pos_ref, freq_ref, g_mix_ref, wa_ref, wsb_ref, gq_ref, wq_ref,
                 gkv_ref, wk_ref, wv_ref,
                 qm_ref, km_ref, vm_ref, qs_ref, ks_ref, vs_ref):
    n = _rms(x_ref[...], g_mix_ref[...]).astype(BF16)
    pa = _dot(n, wa_ref[...])
    sb = _dot(n, wsb_ref[...])
    c_q = pa[:, :MLA_Q_RANK]
    c_kv = pa[:, MLA_Q_RANK:MLA_Q_RANK + MLA_KV_RANK]
    kpe = pa[:, MLA_Q_RANK + MLA_KV_RANK:]

    ang = pos_ref[...].astype(F32) * freq_ref[...]
    lane = lax.broadcasted_iota(jnp.int32, ang.shape, 1)
    cos_t = jnp.where(lane < MLA_NOPE + MLA_ROPE, jnp.cos(ang), 0.0)
    sin_t = jnp.sin(ang)

    def rope(tile):
        return tile * cos_t + pltpu.roll(tile, LANES - MLA_ROPE, 1) * sin_t

    nq = _rms(c_q, gq_ref[...]).astype(BF16)
    q = _dot(nq, wq_ref[...])
    nkv = _rms(c_kv, gkv_ref[...]).astype(BF16)
    kn = _dot(nkv, wk_ref[...])
    k_pe = rope(kpe)
    for h in range(MLA_HEADS):
        sl = slice(h * LANES, (h + 1) * LANES)
        qm_ref[:, sl] = (rope(q[:, sl]) * (MLA_SCALE * LOG2E)).astype(BF16)
        km_ref[:, sl] = (kn[:, sl] + k_pe).astype(BF16)
    vm_ref[...] = _dot(nkv, wv_ref[...]).astype(BF16).T

    w = SB_HEADS * SB_DIM
    qs_ref[...] = (sb[:, :w] * (SB_SCALE * LOG2E)).astype(BF16)
    ks_ref[...] = sb[:, w:2 * w].astype(BF16)
    vs_ref[...] = sb[:, 2 * w:].astype(BF16).T


def _const_spec(shape):
    return pl.BlockSpec(shape, lambda *_: (0,) * len(shape), pipeline_mode=pl.Buffered(1))


def _proj_call(x2, pos2, freq, g_mix, wa, wsb, gq, wq, gkv, wk, wv, *, tm):
    T, D = x2.shape
    row = lambda w: pl.BlockSpec((tm, w), lambda i: (i, 0))
    consts = (freq, g_mix, wa, wsb, gq, wq, gkv, wk, wv)
    col = lambda w: pl.BlockSpec((w, tm), lambda i: (0, i))
    w_mla, w_v, w_sb = MLA_HEADS * LANES, MLA_HEADS * MLA_V, SB_HEADS * SB_DIM
    tok = lambda w: jax.ShapeDtypeStruct((T, w), BF16)
    tok_t = lambda w: jax.ShapeDtypeStruct((w, T), BF16)
    return pl.pallas_call(
        _proj_kernel,
        grid=(T // tm,),
        in_specs=[row(D), row(1)] + [_const_spec(c.shape) for c in consts],
        out_specs=[row(w_mla), row(w_mla), col(w_v), row(w_sb), row(w_sb), col(w_sb)],
        out_shape=[tok(w_mla), tok(w_mla), tok_t(w_v), tok(w_sb), tok(w_sb), tok_t(w_sb)],
        compiler_params=pltpu.CompilerParams(dimension_semantics=("parallel",),
                                             vmem_limit_bytes=VMEM_LIMIT),
        name="proj",
    )(x2, pos2, *consts)


ONES_ROWS = 16


def _mla_kernel(q_ref, k_ref, vt_ref, o_ref, acc_ref, m_ref, qt_ref, p_ref, alpha_ref, *, t, hp, wide):
    qi = pl.program_id(2)
    acc_ref[...] = jnp.zeros_like(acc_ref)
    m_ref[...] = jnp.full_like(m_ref, NEG_INF)
    for h in range(hp):
        qt_ref[h] = q_ref[:, h * LANES:(h + 1) * LANES].astype(F32).T.astype(BF16)
    key_l = lax.broadcasted_iota(jnp.int32, (t, t), 0)
    qry_l = lax.broadcasted_iota(jnp.int32, (t, t), 1)

    def scores(j, n_blk):
        k0 = pl.multiple_of(j * t, t)
        return [_dot(k_ref[pl.ds(k0, n_blk * t), h * LANES:(h + 1) * LANES], qt_ref[h]) for h in range(hp)]

    def softmax(h, s, masked):
        if masked:
            s = jnp.where(key_l <= qry_l, s, NEG_INF)
        m_prev = m_ref[h]
        m_new = jnp.maximum(m_prev, jnp.max(s, axis=0, keepdims=True))
        m_ref[h] = m_new
        return jnp.exp2(s - m_new).astype(BF16), jnp.exp2(m_prev - m_new)

    def pv(h, j, n_blk, p, alpha):
        k0 = pl.multiple_of(j * t, t)
        w = n_blk * t
        v_aug = jnp.concatenate([vt_ref[h * MLA_V:(h + 1) * MLA_V, pl.ds(k0, w)],
                                 jnp.ones((ONES_ROWS, w), BF16)], axis=0)
        acc_ref[h] = alpha * acc_ref[h] + _dot(v_aug, p)

    def step(j, n_blk, masked):
        s_all = scores(j, n_blk)
        for h in range(hp):
            p, alpha = softmax(h, s_all[h], masked)
            pv(h, j, n_blk, p, alpha)

    def refill(g, slot, s_all):
        for h in range(hp):
            p_ref[slot, h], alpha_ref[slot, h] = softmax(h, s_all[h], False)

    def back(g, slot):
        for h in range(hp):
            pv(h, g * wide, wide, p_ref[slot, h], alpha_ref[slot, h])

    def pipelined(g, slot):
        s_all = scores(g * wide, wide)
        back(g - 1, 1 - slot)
        refill(g, slot, s_all)

    def body(i, c):
        pipelined(2 * i + 1, 1)
        pipelined(2 * i + 2, 0)
        return c

    n_wide = qi // wide
    n_pairs = jnp.maximum(n_wide - 1, 0) // 2

    @pl.when(n_wide >= 1)
    def _():
        refill(0, 0, scores(0, wide))

    lax.fori_loop(0, n_pairs, body, 0)

    last_is_odd = jnp.logical_and(n_wide >= 2, n_wide % 2 == 0)

    @pl.when(last_is_odd)
    def _():
        pipelined(n_wide - 1, 1)

    @pl.when(last_is_odd)
    def _():
        back(n_wide - 1, 1)

    @pl.when(n_wide % 2 == 1)
    def _():
        back(n_wide - 1, 0)

    for r in range(wide - 1):
        @pl.when(qi - n_wide * wide > r)
        def _():
            step(n_wide * wide + r, 1, False)
    step(qi, 1, True)

    for pair in range(hp // HEAD_PAIR):
        outs = []
        for h in (pair * HEAD_PAIR, pair * HEAD_PAIR + 1):
            a = acc_ref[h]
            outs.append(a[:MLA_V] / a[MLA_V:MLA_V + 1])
        o_t = jnp.concatenate(outs, axis=0)
        o_ref[:, pair * LANES:(pair + 1) * LANES] = o_t.T.astype(o_ref.dtype)


def _mla_call(qm, km, vmt, *, t, hp, wide):
    B, S, _ = qm.shape
    return pl.pallas_call(
        functools.partial(_mla_kernel, t=t, hp=hp, wide=wide),
        grid=(B, MLA_HEADS // hp, S // t),
        in_specs=[pl.BlockSpec((None, t, hp * LANES), lambda b, g, qi: (b, qi, g)),
                  pl.BlockSpec((None, S, hp * LANES), lambda b, g, qi: (b, 0, g)),
                  pl.BlockSpec((hp * MLA_V, S), lambda b, g, qi: (g, b))],
        out_specs=pl.BlockSpec((None, t, hp * MLA_V), lambda b, g, qi: (b, qi, g)),
        out_shape=jax.ShapeDtypeStruct((B, S, MLA_HEADS * MLA_V), BF16),
        scratch_shapes=[pltpu.VMEM((hp, MLA_V + ONES_ROWS, t), F32),
                        pltpu.VMEM((hp, 1, t), F32),
                        pltpu.VMEM((hp, LANES, t), BF16),
                        pltpu.VMEM((2, hp, wide * t, t), BF16),
                        pltpu.VMEM((2, hp, 1, t), F32)],
        compiler_params=pltpu.CompilerParams(dimension_semantics=("parallel", "parallel", "arbitrary"),
                                             vmem_limit_bytes=VMEM_LIMIT),
        name="mla_attn",
    )(qm, km, vmt)


def _sb_kernel(q_ref, k_ref, vt_ref, tri_ref, o_ref, acc_ref, carry_ref, qt_ref, a_ref, tot_ref, *, t, hp, wide):
    qi = pl.program_id(2)
    acc_ref[...] = jnp.zeros_like(acc_ref)
    carry_ref[...] = jnp.zeros_like(carry_ref)
    q_lane = lax.broadcasted_iota(jnp.int32, (t, LANES), 1)
    for pair in range(hp // HEAD_PAIR):
        q = q_ref[:, pair * LANES:(pair + 1) * LANES].astype(F32)
        qt_ref[pair * HEAD_PAIR] = jnp.where(q_lane < SB_DIM, q, 0.0).T.astype(BF16)
        qt_ref[pair * HEAD_PAIR + 1] = jnp.where(q_lane >= SB_DIM, q, 0.0).T.astype(BF16)
    key_l = lax.broadcasted_iota(jnp.int32, (t, t), 0)
    qry_l = lax.broadcasted_iota(jnp.int32, (t, t), 1)

    valid = key_l < qry_l

    def key_offsets(j_top, n_blk):
        return [pl.multiple_of((j_top - b) * t, t) for b in range(n_blk)]

    def scores(j_top, n_blk):
        zs = {}
        for b, k0 in enumerate(key_offsets(j_top, n_blk)):
            for h in range(hp):
                pair = h // HEAD_PAIR
                k_blk = k_ref[pl.ds(k0, t), pair * LANES:(pair + 1) * LANES]
                zs[b, h] = _dot(k_blk, qt_ref[h])
        return zs

    def suffix(z, masked):
        sp = jnp.maximum(z, jnp.log2(1.0 + jnp.exp2(jnp.minimum(z, SOFTPLUS_CLAMP))))
        if masked:
            sp = jnp.where(valid, sp, 0.0)
        c = _dot(tri_ref[...], sp.astype(BF16))
        a = jnp.exp2(z - c)
        if masked:
            a = jnp.where(valid, a, 0.0)
        return a.astype(BF16), c[0:1, :]

    def weights_pv(h, k0, a, total):
        vt = vt_ref[h * SB_DIM:(h + 1) * SB_DIM, pl.ds(k0, t)]
        carry = carry_ref[h]
        acc_ref[h] += _dot(vt, a) * jnp.exp2(carry)
        carry_ref[h] = carry - total

    def step(j_top, n_blk, masked):
        zs = scores(j_top, n_blk)
        mid = {bh: suffix(z, masked) for bh, z in zs.items()}
        for b, k0 in enumerate(key_offsets(j_top, n_blk)):
            for h in range(hp):
                weights_pv(h, k0, *mid[b, h])

    def group_top(g):
        return qi - 2 - g * wide

    def refill(g, zs):
        for (b, h), z in zs.items():
            a_ref[g % 2, b, h], tot_ref[g % 2, b, h] = suffix(z, False)

    def back(g):
        for b, k0 in enumerate(key_offsets(group_top(g), wide)):
            for h in range(hp):
                weights_pv(h, k0, a_ref[g % 2, b, h], tot_ref[g % 2, b, h])

    def alive():
        return (jnp.max(carry_ref[...]) > EXP2_IS_ZERO_BELOW).astype(jnp.int32)

    step(qi, 1, True)

    @pl.when(qi >= 1)
    def _():
        step(qi - 1, 1, False)

    n_rest = jnp.maximum(qi - 1, 0)
    n_wide = n_rest // wide

    @pl.when(jnp.logical_and(n_wide >= 1, alive() == 1))
    def _():
        refill(0, scores(group_top(0), wide))

    def cond(state):
        g, live = state
        return jnp.logical_and(g < n_wide, live == 1)

    def body(state):
        g, _ = state
        zs = scores(group_top(g), wide)
        back(g - 1)
        refill(g, zs)
        return g + 1, alive()

    g_end, live = lax.while_loop(cond, body, (jnp.int32(1), alive()))

    @pl.when(jnp.logical_and(n_wide >= 1, live == 1))
    def _():
        back(g_end - 1)

    for r in range(wide - 1):
        @pl.when(jnp.logical_and(n_rest - n_wide * wide > r, alive() == 1))
        def _():
            step(qi - 2 - n_wide * wide - r, 1, False)

    for pair in range(hp // HEAD_PAIR):
        o_t = jnp.concatenate([acc_ref[pair * HEAD_PAIR], acc_ref[pair * HEAD_PAIR + 1]], axis=0)
        o_ref[:, pair * LANES:(pair + 1) * LANES] = o_t.T.astype(o_ref.dtype)


def _sb_call(qs, ks, vst, *, t, hp, wide):
    B, S, _ = qs.shape
    r = lax.broadcasted_iota(jnp.int32, (t, t), 0)
    c = lax.broadcasted_iota(jnp.int32, (t, t), 1)
    tri = (c >= r).astype(BF16)
    w = hp * SB_DIM
    return pl.pallas_call(
        functools.partial(_sb_kernel, t=t, hp=hp, wide=wide),
        grid=(B, SB_HEADS // hp, S // t),
        in_specs=[pl.BlockSpec((None, t, w), lambda b, g, qi: (b, qi, g)),
                  pl.BlockSpec((None, S, w), lambda b, g, qi: (b, 0, g)),
                  pl.BlockSpec((w, S), lambda b, g, qi: (g, b)),
                  pl.BlockSpec((t, t), lambda b, g, qi: (0, 0))],
        out_specs=pl.BlockSpec((None, t, w), lambda b, g, qi: (b, qi, g)),
        out_shape=jax.ShapeDtypeStruct((B, S, SB_HEADS * SB_DIM), BF16),
        scratch_shapes=[pltpu.VMEM((hp, SB_DIM, t), F32),
                        pltpu.VMEM((hp, 1, t), F32),
                        pltpu.VMEM((hp, LANES, t), BF16),
                        pltpu.VMEM((2, wide, hp, t, t), BF16),
                        pltpu.VMEM((2, wide, hp, 1, t), F32)],
        compiler_params=pltpu.CompilerParams(dimension_semantics=("parallel", "parallel", "arbitrary"),
                                             vmem_limit_bytes=VMEM_LIMIT),
        name="sb_attn",
    )(qs, ks, vst, tri)


def _post_kernel(x_ref, oa_ref, ob_ref, p_ref, g_mix_ref, wga_ref, wgb_ref, wbra_ref, wbrb_ref, wout_ref,
                 g_ffn_ref, wfg_ref, wfu_ref, wfd_ref, wpg_ref, wpp_ref, g_ple_ref, g_fin_ref, out_ref,
                 *, ff_chunk, final_norm):
    x = x_ref[...]
    n = _rms(x, g_mix_ref[...]).astype(BF16)
    merged = (jax.nn.sigmoid(_dot(n, wga_ref[...])) * _dot(oa_ref[...], wbra_ref[...])
              + jax.nn.sigmoid(_dot(n, wgb_ref[...])) * _dot(ob_ref[...], wbrb_ref[...]))
    h = x + _dot(merged.astype(BF16), wout_ref[...])

    n2 = _rms(h, g_ffn_ref[...]).astype(BF16)
    d_ff = wfg_ref.shape[1]
    ff = None
    for c0 in range(0, d_ff, ff_chunk):
        c1 = min(c0 + ff_chunk, d_ff)
        g = _dot(n2, wfg_ref[:, c0:c1])
        u = _dot(n2, wfu_ref[:, c0:c1])
        part = _dot((g * jax.nn.sigmoid(g) * u).astype(BF16), wfd_ref[c0:c1, :])
        ff = part if ff is None else ff + part
    h = h + ff

    e = _rms(_dot(p_ref[...].astype(BF16), wpp_ref[...]), g_ple_ref[...])
    h = h + jax.nn.sigmoid(_dot(h.astype(BF16), wpg_ref[...])) * e
    out_ref[...] = _rms(h, g_fin_ref[...]) if final_norm else h


def _post_call(x2, oa, ob, p2, consts, *, tm, ff_chunk, final_norm):
    T, D = x2.shape
    row = lambda w: pl.BlockSpec((tm, w), lambda i: (i, 0))
    return pl.pallas_call(
        functools.partial(_post_kernel, ff_chunk=ff_chunk, final_norm=final_norm),
        grid=(T // tm,),
        in_specs=[row(D), row(oa.shape[1]), row(ob.shape[1]), row(p2.shape[1])]
                 + [_const_spec(c.shape) for c in consts],
        out_specs=row(D),
        out_shape=jax.ShapeDtypeStruct((T, D), F32),
        compiler_params=pltpu.CompilerParams(dimension_semantics=("parallel",),
                                             vmem_limit_bytes=VMEM_LIMIT),
        name="post",
    )(x2, oa, ob, p2, *consts)


def _rotate_half_cols(w):
    half = w.shape[-1] // 2
    return jnp.concatenate([-w[..., half:], w[..., :half]], axis=-1)


def _layer_weights(w_in, w_q_b, w_kv_b):
    d = w_in.shape[0]
    o = 0
    cols = []
    for wd in (MLA_Q_RANK, MLA_KV_RANK, MLA_ROPE, SB_HEADS * SB_DIM, SB_HEADS * SB_DIM, SB_HEADS * SB_DIM, d, d):
        cols.append(w_in[:, o:o + wd])
        o += wd
    w_cq, w_ckv, w_kpe, w_qs, w_ks, w_vs, w_ga, w_gb = cols

    kpe_tile = jnp.concatenate([jnp.zeros_like(w_in[:, :MLA_NOPE]), w_kpe, _rotate_half_cols(w_kpe)], axis=1)
    wa = jnp.concatenate([w_cq, w_ckv, kpe_tile], axis=1)
    wsb = jnp.concatenate([w_qs, w_ks, w_vs], axis=1)

    qb = w_q_b.reshape(MLA_Q_RANK, MLA_HEADS, MLA_NOPE + MLA_ROPE)
    wq = jnp.concatenate([qb, _rotate_half_cols(qb[..., MLA_NOPE:])], axis=-1).reshape(MLA_Q_RANK, MLA_HEADS * LANES)

    kvb = w_kv_b.reshape(MLA_KV_RANK, MLA_HEADS, MLA_NOPE + MLA_V)
    wk = jnp.pad(kvb[..., :MLA_NOPE], ((0, 0), (0, 0), (0, LANES - MLA_NOPE))).reshape(MLA_KV_RANK, MLA_HEADS * LANES)
    wv = kvb[..., MLA_NOPE:].reshape(MLA_KV_RANK, MLA_HEADS * MLA_V)
    bf = lambda t: t.astype(BF16)
    return tuple(map(bf, (wa, wsb, wq, wk, wv, w_ga, w_gb)))


def _rope_freq_row():
    inv_freq = 1.0 / (ROPE_THETA ** (jnp.arange(0, MLA_ROPE, 2, dtype=F32) / MLA_ROPE))
    both = jnp.concatenate([inv_freq, inv_freq])
    return jnp.pad(both, (MLA_NOPE, LANES - MLA_NOPE - MLA_ROPE)).reshape(1, LANES)


def _tiles(B, S):
    T = B * S
    tm_proj = min(512, T)
    tm_post = min(512, T)
    t_attn = min(256, S)
    heads_per_step = 8
    wide = 2
    return tm_proj, tm_post, t_attn, heads_per_step, wide


def kernel(x, p, positions, g_mix, w_in, g_q_a, w_q_b, g_kv_a, w_kv_b, w_br_mla, w_br_sb, w_out, g_ffn,
           w_ffn_gate, w_ffn_up, w_ffn_down, w_ple_gate, w_ple_proj, g_ple, g_final):
    B, S, D = x.shape
    T = B * S
    depth = w_in.shape[0]
    tm_proj, tm_post, t_attn, hp, wide = _tiles(B, S)
    d_ff = w_ffn_gate.shape[-1]
    ff_chunk = min(d_ff, 1024)
    row = lambda g: g.reshape(1, -1).astype(F32)
    bf = lambda t: t.astype(BF16)

    pos2 = positions.reshape(T, 1).astype(jnp.int32)
    freq = _rope_freq_row()
    h = x.reshape(T, D)
    for i in range(depth):
        wa, wsb, wq, wk, wv, w_ga, w_gb = _layer_weights(w_in[i], w_q_b[i], w_kv_b[i])
        qm, km, vm, qs, ks, vs = _proj_call(h, pos2, freq, row(g_mix[i]), wa, wsb, row(g_q_a[i]), wq,
                                            row(g_kv_a[i]), wk, wv, tm=tm_proj)
        shp = lambda t: t.reshape(B, S, t.shape[-1])
        o_a = _mla_call(shp(qm), shp(km), vm, t=t_attn, hp=hp, wide=wide).reshape(T, -1)
        o_b = _sb_call(shp(qs), shp(ks), vs, t=t_attn, hp=hp, wide=wide).reshape(T, -1)
        consts = (row(g_mix[i]), w_ga, w_gb, bf(w_br_mla[i]), bf(w_br_sb[i]), bf(w_out[i]), row(g_ffn[i]),
                  bf(w_ffn_gate[i]), bf(w_ffn_up[i]), bf(w_ffn_down[i]), bf(w_ple_gate[i]), bf(w_ple_proj[i]),
                  row(g_ple[i]), row(g_final))
        h = _post_call(h, o_a, o_b, p[i].reshape(T, -1), consts, tm=tm_post, ff_chunk=ff_chunk,
                       final_norm=(i == depth - 1))
    return h.reshape(B, S, D)
```

```python
import functools
import math

import jax
import jax.numpy as jnp
from jax import lax
from jax.experimental import pallas as pl
from jax.experimental.pallas import tpu as pltpu

EPS = 1e-6
MLA_HEADS = 8
MLA_NOPE = 64
MLA_ROPE = 32
MLA_V = 64
MLA_Q_RANK = 384
MLA_KV_RANK = 256
ROPE_THETA = 10000.0
MLA_SCALE = 1.0 / math.sqrt(MLA_NOPE + MLA_ROPE)
SB_HEADS = 8
SB_DIM = 64
SB_SCALE = 1.0 / math.sqrt(SB_DIM)
NEG_INF = -1e30
LOG2E = math.log2(math.e)
SOFTPLUS_CLAMP = 64.0
EXP2_IS_ZERO_BELOW = -151.0

LANES = 128
HEAD_PAIR = 2
VMEM_LIMIT = 56 * 1024 * 1024

F32 = jnp.float32
BF16 = jnp.bfloat16


def _rms(x, g):
    return x * lax.rsqrt(jnp.mean(x * x, axis=-1, keepdims=True) + EPS) * g


def _dot(a, b):
    return jnp.dot(a, b, preferred_element_type=F32)


def _dot_nt(a, b):
    return lax.dot_general(a, b, (((1,), (1,)), ((), ())), preferred_element_type=F32)


def _proj_kernel(x_ref, pos_ref, freq_ref, g_mix_ref, wa_ref, wsb_ref, gq_ref, wq_ref,
                 gkv_ref, wk_ref, wv_ref,
                 qm_ref, km_ref, vm_ref, qs_ref, ks_ref, vs_ref):
    n = _rms(x_ref[...], g_mix_ref[...]).astype(BF16)
    pa = _dot(n, wa_ref[...])
    sb = _dot(n, wsb_ref[...])
    c_q = pa[:, :MLA_Q_RANK]
    c_kv = pa[:, MLA_Q_RANK:MLA_Q_RANK + MLA_KV_RANK]
    kpe = pa[:, MLA_Q_RANK + MLA_KV_RANK:]

    ang = pos_ref[...].astype(F32) * freq_ref[...]
    lane = lax.broadcasted_iota(jnp.int32, ang.shape, 1)
    cos_t = jnp.where(lane < MLA_NOPE + MLA_ROPE, jnp.cos(ang), 0.0)
    sin_t = jnp.sin(ang)

    def rope(tile):
        return tile * cos_t + pltpu.roll(tile, LANES - MLA_ROPE, 1) * sin_t

    nq = _rms(c_q, gq_ref[...]).astype(BF16)
    q = _dot(nq, wq_ref[...])
    nkv = _rms(c_kv, gkv_ref[...]).astype(BF16)
    kn = _dot(nkv, wk_ref[...])
    k_pe = rope(kpe)
    for h in range(MLA_HEADS):
        sl = slice(h * LANES, (h + 1) * LANES)
        qm_ref[:, sl] = (rope(q[:, sl]) * (MLA_SCALE * LOG2E)).astype(BF16)
        km_ref[:, sl] = (kn[:, sl] + k_pe).astype(BF16)
    vm_ref[...] = _dot(nkv, wv_ref[...]).astype(BF16).T

    w = SB_HEADS * SB_DIM
    qs_ref[...] = (sb[:, :w] * (SB_SCALE * LOG2E)).astype(BF16)
    ks_ref[...] = sb[:, w:2 * w].astype(BF16)
    vs_ref[...] = sb[:, 2 * w:].astype(BF16).T


def _const_spec(shape):
    return pl.BlockSpec(shape, lambda *_: (0,) * len(shape), pipeline_mode=pl.Buffered(1))


def _proj_call(x2, pos2, freq, g_mix, wa, wsb, gq, wq, gkv, wk, wv, *, tm):
    T, D = x2.shape
    row = lambda w: pl.BlockSpec((tm, w), lambda i: (i, 0))
    consts = (freq, g_mix, wa, wsb, gq, wq, gkv, wk, wv)
    col = lambda w: pl.BlockSpec((w, tm), lambda i: (0, i))
    w_mla, w_v, w_sb = MLA_HEADS * LANES, MLA_HEADS * MLA_V, SB_HEADS * SB_DIM
    tok = lambda w: jax.ShapeDtypeStruct((T, w), BF16)
    tok_t = lambda w: jax.ShapeDtypeStruct((w, T), BF16)
    return pl.pallas_call(
        _proj_kernel,
        grid=(T // tm,),
        in_specs=[row(D), row(1)] + [_const_spec(c.shape) for c in consts],
        out_specs=[row(w_mla), row(w_mla), col(w_v), row(w_sb), row(w_sb), col(w_sb)],
        out_shape=[tok(w_mla), tok(w_mla), tok_t(w_v), tok(w_sb), tok(w_sb), tok_t(w_sb)],
        compiler_params=pltpu.CompilerParams(dimension_semantics=("parallel",),
                                             vmem_limit_bytes=VMEM_LIMIT),
        name="proj",
    )(x2, pos2, *consts)


ONES_ROWS = 16
MAX_LAG = 64.0


def _mla_kernel(q_ref, k_ref, vt_ref, o_ref, acc_ref, m_ref, qt_ref, p_ref, scale_ref, lag_ref, *, t, hp, wide):
    qi = pl.program_id(2)
    for h in range(hp):
        qt_ref[h] = q_ref[:, h * LANES:(h + 1) * LANES].astype(F32).T.astype(BF16)
    key_l = lax.broadcasted_iota(jnp.int32, (t, t), 0)
    qry_l = lax.broadcasted_iota(jnp.int32, (t, t), 1)

    def scores(j, n_blk):
        k0 = pl.multiple_of(j * t, t)
        return [_dot(k_ref[pl.ds(k0, n_blk * t), h * LANES:(h + 1) * LANES], qt_ref[h]) for h in range(hp)]

    def values(h, j, n_blk):
        k0 = pl.multiple_of(j * t, t)
        w = n_blk * t
        return jnp.concatenate([vt_ref[h * MLA_V:(h + 1) * MLA_V, pl.ds(k0, w)],
                                jnp.ones((ONES_ROWS, w), BF16)], axis=0)

    def one_pass(h, s, masked, first):
        if masked:
            s = jnp.where(key_l <= qry_l, s, NEG_INF)
        key0 = s[0:1, :]
        if first is True:
            m_old = key0
        elif first is False:
            m_old = m_ref[h]
        else:
            m_old = jnp.where(first, key0, m_ref[h])
        p = jnp.exp2(s - m_old).astype(BF16)
        lag = jnp.maximum(jnp.max(s, axis=0, keepdims=True) - m_old, 0.0)
        m_ref[h] = m_old + lag
        lag_ref[h] = jnp.maximum(lag_ref[h], lag)
        return p, jnp.exp2(-lag)

    def add_pv(h, j, n_blk, p, scale):
        acc_ref[h] = (acc_ref[h] + _dot(values(h, j, n_blk), p)) * scale

    def step(j, n_blk, masked, first):
        s_all = scores(j, n_blk)
        for h in range(hp):
            add_pv(h, j, n_blk, *one_pass(h, s_all[h], masked, first))

    def refill(g, slot, s_all, first):
        for h in range(hp):
            p_ref[slot, h], scale_ref[slot, h] = one_pass(h, s_all[h], False, first)

    def back(g, slot):
        for h in range(hp):
            add_pv(h, g * wide, wide, p_ref[slot, h], scale_ref[slot, h])

    def pipelined(g, slot):
        s_all = scores(g * wide, wide)
        back(g - 1, 1 - slot)
        refill(g, slot, s_all, False)

    def body(i, c):
        pipelined(2 * i + 1, 1)
        pipelined(2 * i + 2, 0)
        return c

    acc_ref[...] = jnp.zeros_like(acc_ref)
    m_ref[...] = jnp.zeros_like(m_ref)
    lag_ref[...] = jnp.zeros_like(lag_ref)
    n_wide = qi // wide
    n_pairs = jnp.maximum(n_wide - 1, 0) // 2

    @pl.when(n_wide >= 1)
    def _():
        refill(0, 0, scores(0, wide), True)

    lax.fori_loop(0, n_pairs, body, 0)

    last_is_odd = jnp.logical_and(n_wide >= 2, n_wide % 2 == 0)

    @pl.when(last_is_odd)
    def _():
        pipelined(n_wide - 1, 1)

    @pl.when(last_is_odd)
    def _():
        back(n_wide - 1, 1)

    @pl.when(n_wide % 2 == 1)
    def _():
        back(n_wide - 1, 0)

    for r in range(wide - 1):
        @pl.when(qi - n_wide * wide > r)
        def _():
            step(n_wide * wide + r, 1, False, jnp.logical_and(n_wide == 0, r == 0))
    step(qi, 1, True, qi == 0)

    def two_pass_step(j, masked):
        s_all = scores(j, 1)
        for h in range(hp):
            s = s_all[h]
            if masked:
                s = jnp.where(key_l <= qry_l, s, NEG_INF)
            m_prev = m_ref[h]
            m_new = jnp.maximum(m_prev, jnp.max(s, axis=0, keepdims=True))
            m_ref[h] = m_new
            acc_ref[h] = jnp.exp2(m_prev - m_new) * acc_ref[h] + _dot(values(h, j, 1), jnp.exp2(s - m_new).astype(BF16))

    @pl.when(jnp.max(lag_ref[...]) > MAX_LAG)
    def _():
        acc_ref[...] = jnp.zeros_like(acc_ref)
        m_ref[...] = jnp.full_like(m_ref, NEG_INF)
        lax.fori_loop(0, qi, lambda j, c: (two_pass_step(j, False), c)[1], 0)
        two_pass_step(qi, True)

    for pair in range(hp // HEAD_PAIR):
        outs = []
        for h in (pair * HEAD_PAIR, pair * HEAD_PAIR + 1):
            a = acc_ref[h]
            outs.append(a[:MLA_V] / a[MLA_V:MLA_V + 1])
        o_t = jnp.concatenate(outs, axis=0)
        o_ref[:, pair * LANES:(pair + 1) * LANES] = o_t.T.astype(o_ref.dtype)


def _mla_call(qm, km, vmt, *, t, hp, wide):
    B, S, _ = qm.shape
    return pl.pallas_call(
        functools.partial(_mla_kernel, t=t, hp=hp, wide=wide),
        grid=(B, MLA_HEADS // hp, S // t),
        in_specs=[pl.BlockSpec((None, t, hp * LANES), lambda b, g, qi: (b, qi, g)),
                  pl.BlockSpec((None, S, hp * LANES), lambda b, g, qi: (b, 0, g)),
                  pl.BlockSpec((hp * MLA_V, S), lambda b, g, qi: (g, b))],
        out_specs=pl.BlockSpec((None, t, hp * MLA_V), lambda b, g, qi: (b, qi, g)),
        out_shape=jax.ShapeDtypeStruct((B, S, MLA_HEADS * MLA_V), BF16),
        scratch_shapes=[pltpu.VMEM((hp, MLA_V + ONES_ROWS, t), F32),
                        pltpu.VMEM((hp, 1, t), F32),
                        pltpu.VMEM((hp, LANES, t), BF16),
                        pltpu.VMEM((2, hp, wide * t, t), BF16),
                        pltpu.VMEM((2, hp, 1, t), F32),
                        pltpu.VMEM((hp, 1, t), F32)],
        compiler_params=pltpu.CompilerParams(dimension_semantics=("parallel", "parallel", "arbitrary"),
                                             vmem_limit_bytes=VMEM_LIMIT),
        name="mla_attn",
    )(qm, km, vmt)


def _sb_kernel(q_ref, k_ref, vt_ref, tri_ref, o_ref, acc_ref, carry_ref, qt_ref, a_ref, tot_ref, *, t, hp, wide):
    qi = pl.program_id(2)
    acc_ref[...] = jnp.zeros_like(acc_ref)
    carry_ref[...] = jnp.zeros_like(carry_ref)
    q_lane = lax.broadcasted_iota(jnp.int32, (t, LANES), 1)
    for pair in range(hp // HEAD_PAIR):
        q = q_ref[:, pair * LANES:(pair + 1) * LANES].astype(F32)
        qt_ref[pair * HEAD_PAIR] = jnp.where(q_lane < SB_DIM, q, 0.0).T.astype(BF16)
        qt_ref[pair * HEAD_PAIR + 1] = jnp.where(q_lane >= SB_DIM, q, 0.0).T.astype(BF16)
    key_l = lax.broadcasted_iota(jnp.int32, (t, t), 0)
    qry_l = lax.broadcasted_iota(jnp.int32, (t, t), 1)

    valid = key_l < qry_l

    def key_offsets(j_top, n_blk):
        return [pl.multiple_of((j_top - b) * t, t) for b in range(n_blk)]

    def scores(j_top, n_blk):
        zs = {}
        for b, k0 in enumerate(key_offsets(j_top, n_blk)):
            for h in range(hp):
                pair = h // HEAD_PAIR
                k_blk = k_ref[pl.ds(k0, t), pair * LANES:(pair + 1) * LANES]
                zs[b, h] = _dot(k_blk, qt_ref[h])
        return zs

    def suffix(z, masked):
        sp = jnp.maximum(z, jnp.log2(1.0 + jnp.exp2(jnp.minimum(z, SOFTPLUS_CLAMP))))
        if masked:
            sp = jnp.where(valid, sp, 0.0)
        c = _dot(tri_ref[...], sp.astype(BF16))
        a = jnp.exp2(z - c)
        if masked:
            a = jnp.where(valid, a, 0.0)
        return a.astype(BF16), c[0:1, :]

    def weights_pv(h, k0, a, total):
        vt = vt_ref[h * SB_DIM:(h + 1) * SB_DIM, pl.ds(k0, t)]
        carry = carry_ref[h]
        acc_ref[h] += _dot(vt, a) * jnp.exp2(carry)
        carry_ref[h] = carry - total

    def step(j_top, n_blk, masked):
        zs = scores(j_top, n_blk)
        mid = {bh: suffix(z, masked) for bh, z in zs.items()}
        for b, k0 in enumerate(key_offsets(j_top, n_blk)):
            for h in range(hp):
                weights_pv(h, k0, *mid[b, h])

    def group_top(g):
        return qi - 2 - g * wide

    def refill(g, zs):
        for (b, h), z in zs.items():
            a_ref[g % 2, b, h], tot_ref[g % 2, b, h] = suffix(z, False)

    def back(g):
        for b, k0 in enumerate(key_offsets(group_top(g), wide)):
            for h in range(hp):
                weights_pv(h, k0, a_ref[g % 2, b, h], tot_ref[g % 2, b, h])

    def alive():
        return (jnp.max(carry_ref[...]) > EXP2_IS_ZERO_BELOW).astype(jnp.int32)

    step(qi, 1, True)

    @pl.when(qi >= 1)
    def _():
        step(qi - 1, 1, False)

    n_rest = jnp.maximum(qi - 1, 0)
    n_wide = n_rest // wide

    @pl.when(jnp.logical_and(n_wide >= 1, alive() == 1))
    def _():
        refill(0, scores(group_top(0), wide))

    def cond(state):
        g, live = state
        return jnp.logical_and(g < n_wide, live == 1)

    def body(state):
        g, _ = state
        zs = scores(group_top(g), wide)
        back(g - 1)
        refill(g, zs)
        return g + 1, alive()

    g_end, live = lax.while_loop(cond, body, (jnp.int32(1), alive()))

    @pl.when(jnp.logical_and(n_wide >= 1, live == 1))
    def _():
        back(g_end - 1)

    for r in range(wide - 1):
        @pl.when(jnp.logical_and(n_rest - n_wide * wide > r, alive() == 1))
        def _():
            step(qi - 2 - n_wide * wide - r, 1, False)

    for pair in range(hp // HEAD_PAIR):
        o_t = jnp.concatenate([acc_ref[pair * HEAD_PAIR], acc_ref[pair * HEAD_PAIR + 1]], axis=0)
        o_ref[:, pair * LANES:(pair + 1) * LANES] = o_t.T.astype(o_ref.dtype)


def _sb_call(qs, ks, vst, *, t, hp, wide):
    B, S, _ = qs.shape
    r = lax.broadcasted_iota(jnp.int32, (t, t), 0)
    c = lax.broadcasted_iota(jnp.int32, (t, t), 1)
    tri = (c >= r).astype(BF16)
    w = hp * SB_DIM
    return pl.pallas_call(
        functools.partial(_sb_kernel, t=t, hp=hp, wide=wide),
        grid=(B, SB_HEADS // hp, S // t),
        in_specs=[pl.BlockSpec((None, t, w), lambda b, g, qi: (b, qi, g)),
                  pl.BlockSpec((None, S, w), lambda b, g, qi: (b, 0, g)),
                  pl.BlockSpec((w, S), lambda b, g, qi: (g, b)),
                  pl.BlockSpec((t, t), lambda b, g, qi: (0, 0))],
        out_specs=pl.BlockSpec((None, t, w), lambda b, g, qi: (b, qi, g)),
        out_shape=jax.ShapeDtypeStruct((B, S, SB_HEADS * SB_DIM), BF16),
        scratch_shapes=[pltpu.VMEM((hp, SB_DIM, t), F32),
                        pltpu.VMEM((hp, 1, t), F32),
                        pltpu.VMEM((hp, LANES, t), BF16),
                        pltpu.VMEM((2, wide, hp, t, t), BF16),
                        pltpu.VMEM((2, wide, hp, 1, t), F32)],
        compiler_params=pltpu.CompilerParams(dimension_semantics=("parallel", "parallel", "arbitrary"),
                                             vmem_limit_bytes=VMEM_LIMIT),
        name="sb_attn",
    )(qs, ks, vst, tri)


def _post_kernel(x_ref, oa_ref, ob_ref, p_ref, g_mix_ref, wga_ref, wgb_ref, wbra_ref, wbrb_ref, wout_ref,
                 g_ffn_ref, wfg_ref, wfu_ref, wfd_ref, wpg_ref, wpp_ref, g_ple_ref, g_fin_ref, out_ref,
                 *, ff_chunk, final_norm):
    x = x_ref[...]
    n = _rms(x, g_mix_ref[...]).astype(BF16)
    merged = (jax.nn.sigmoid(_dot(n, wga_ref[...])) * _dot(oa_ref[...], wbra_ref[...])
              + jax.nn.sigmoid(_dot(n, wgb_ref[...])) * _dot(ob_ref[...], wbrb_ref[...]))
    h = x + _dot(merged.astype(BF16), wout_ref[...])

    n2 = _rms(h, g_ffn_ref[...]).astype(BF16)
    d_ff = wfg_ref.shape[1]
    ff = None
    for c0 in range(0, d_ff, ff_chunk):
        c1 = min(c0 + ff_chunk, d_ff)
        g = _dot(n2, wfg_ref[:, c0:c1])
        u = _dot(n2, wfu_ref[:, c0:c1])
        part = _dot((g * jax.nn.sigmoid(g) * u).astype(BF16), wfd_ref[c0:c1, :])
        ff = part if ff is None else ff + part
    h = h + ff

    e = _rms(_dot(p_ref[...].astype(BF16), wpp_ref[...]), g_ple_ref[...])
    h = h + jax.nn.sigmoid(_dot(h.astype(BF16), wpg_ref[...])) * e
    out_ref[...] = _rms(h, g_fin_ref[...]) if final_norm else h


def _post_call(x2, oa, ob, p2, consts, *, tm, ff_chunk, final_norm):
    T, D = x2.shape
    row = lambda w: pl.BlockSpec((tm, w), lambda i: (i, 0))
    return pl.pallas_call(
        functools.partial(_post_kernel, ff_chunk=ff_chunk, final_norm=final_norm),
        grid=(T // tm,),
        in_specs=[row(D), row(oa.shape[1]), row(ob.shape[1]), row(p2.shape[1])]
                 + [_const_spec(c.shape) for c in consts],
        out_specs=row(D),
        out_shape=jax.ShapeDtypeStruct((T, D), F32),
        compiler_params=pltpu.CompilerParams(dimension_semantics=("parallel",),
                                             vmem_limit_bytes=VMEM_LIMIT),
        name="post",
    )(x2, oa, ob, p2, *consts)


def _rotate_half_cols(w):
    half = w.shape[-1] // 2
    return jnp.concatenate([-w[..., half:], w[..., :half]], axis=-1)


def _layer_weights(w_in, w_q_b, w_kv_b):
    d = w_in.shape[0]
    o = 0
    cols = []
    for wd in (MLA_Q_RANK, MLA_KV_RANK, MLA_ROPE, SB_HEADS * SB_DIM, SB_HEADS * SB_DIM, SB_HEADS * SB_DIM, d, d):
        cols.append(w_in[:, o:o + wd])
        o += wd
    w_cq, w_ckv, w_kpe, w_qs, w_ks, w_vs, w_ga, w_gb = cols

    kpe_tile = jnp.concatenate([jnp.zeros_like(w_in[:, :MLA_NOPE]), w_kpe, _rotate_half_cols(w_kpe)], axis=1)
    wa = jnp.concatenate([w_cq, w_ckv, kpe_tile], axis=1)
    wsb = jnp.concatenate([w_qs, w_ks, w_vs], axis=1)

    qb = w_q_b.reshape(MLA_Q_RANK, MLA_HEADS, MLA_NOPE + MLA_ROPE)
    wq = jnp.concatenate([qb, _rotate_half_cols(qb[..., MLA_NOPE:])], axis=-1).reshape(MLA_Q_RANK, MLA_HEADS * LANES)

    kvb = w_kv_b.reshape(MLA_KV_RANK, MLA_HEADS, MLA_NOPE + MLA_V)
    wk = jnp.pad(kvb[..., :MLA_NOPE], ((0, 0), (0, 0), (0, LANES - MLA_NOPE))).reshape(MLA_KV_RANK, MLA_HEADS * LANES)
    wv = kvb[..., MLA_NOPE:].reshape(MLA_KV_RANK, MLA_HEADS * MLA_V)
    bf = lambda t: t.astype(BF16)
    return tuple(map(bf, (wa, wsb, wq, wk, wv, w_ga, w_gb)))


def _rope_freq_row():
    inv_freq = 1.0 / (ROPE_THETA ** (jnp.arange(0, MLA_ROPE, 2, dtype=F32) / MLA_ROPE))
    both = jnp.concatenate([inv_freq, inv_freq])
    return jnp.pad(both, (MLA_NOPE, LANES - MLA_NOPE - MLA_ROPE)).reshape(1, LANES)


def _tiles(B, S):
    T = B * S
    tm_proj = min(512, T)
    tm_post = min(512, T)
    t_attn = min(256, S)
    heads_per_step = 8
    wide = 2
    return tm_proj, tm_post, t_attn, heads_per_step, wide


def kernel(x, p, positions, g_mix, w_in, g_q_a, w_q_b, g_kv_a, w_kv_b, w_br_mla, w_br_sb, w_out, g_ffn,
           w_ffn_gate, w_ffn_up, w_ffn_down, w_ple_gate, w_ple_proj, g_ple, g_final):
    B, S, D = x.shape
    T = B * S
    depth = w_in.shape[0]
    tm_proj, tm_post, t_attn, hp, wide = _tiles(B, S)
    d_ff = w_ffn_gate.shape[-1]
    ff_chunk = min(d_ff, 1024)
    row = lambda g: g.reshape(1, -1).astype(F32)
    bf = lambda t: t.astype(BF16)

    pos2 = positions.reshape(T, 1).astype(jnp.int32)
    freq = _rope_freq_row()
    h = x.reshape(T, D)
    for i in range(depth):
        wa, wsb, wq, wk, wv, w_ga, w_gb = _layer_weights(w_in[i], w_q_b[i], w_kv_b[i])
        qm, km, vm, qs, ks, vs = _proj_call(h, pos2, freq, row(g_mix[i]), wa, wsb, row(g_q_a[i]), wq,
                                            row(g_kv_a[i]), wk, wv, tm=tm_proj)
        shp = lambda t: t.reshape(B, S, t.shape[-1])
        o_a = _mla_call(shp(qm), shp(km), vm, t=t_attn, hp=hp, wide=wide).reshape(T, -1)
        o_b = _sb_call(shp(qs), shp(ks), vs, t=t_attn, hp=hp, wide=wide).reshape(T, -1)
        consts = (row(g_mix[i]), w_ga, w_gb, bf(w_br_mla[i]), bf(w_br_sb[i]), bf(w_out[i]), row(g_ffn[i]),
                  bf(w_ffn_gate[i]), bf(w_ffn_up[i]), bf(w_ffn_down[i]), bf(w_ple_gate[i]), bf(w_ple_proj[i]),
                  row(g_ple[i]), row(g_final))
        h = _post_call(h, o_a, o_b, p[i].reshape(T, -1), consts, tm=tm_post, ff_chunk=ff_chunk,
                       final_norm=(i == depth - 1))
    return h.reshape(B, S, D)
```

```python
import functools
import math

import jax
import jax.numpy as jnp
from jax import lax
from jax.experimental import pallas as pl
from jax.experimental.pallas import tpu as pltpu

EPS = 1e-6
MLA_HEADS = 8
MLA_NOPE = 64
MLA_ROPE = 32
MLA_V = 64
MLA_Q_RANK = 384
MLA_KV_RANK = 256
ROPE_THETA = 10000.0
MLA_SCALE = 1.0 / math.sqrt(MLA_NOPE + MLA_ROPE)
SB_HEADS = 8
SB_DIM = 64
SB_SCALE = 1.0 / math.sqrt(SB_DIM)
NEG_INF = -1e30
LOG2E = math.log2(math.e)
SOFTPLUS_CLAMP = 64.0
EXP2_IS_ZERO_BELOW = -151.0

LANES = 128
HEAD_PAIR = 2
VMEM_LIMIT = 56 * 1024 * 1024

F32 = jnp.float32
BF16 = jnp.bfloat16


def _rms(x, g):
    return x * lax.rsqrt(jnp.mean(x * x, axis=-1, keepdims=True) + EPS) * g


def _dot(a, b):
    return jnp.dot(a, b, preferred_element_type=F32)


def _dot_nt(a, b):
    return lax.dot_general(a, b, (((1,), (1,)), ((), ())), preferred_element_type=F32)


def _proj_kernel(x_ref, pos_ref, freq_ref, g_mix_ref, wa_ref, wsb_ref, gq_ref, wq_ref,
                 gkv_ref, wk_ref, wv_ref,
                 qm_ref, km_ref, vm_ref, qs_ref, ks_ref, vs_ref):
    n = _rms(x_ref[...], g_mix_ref[...]).astype(BF16)
    pa = _dot(n, wa_ref[...])
    sb = _dot(n, wsb_ref[...])
    c_q = pa[:, :MLA_Q_RANK]
    c_kv = pa[:, MLA_Q_RANK:MLA_Q_RANK + MLA_KV_RANK]
    kpe = pa[:, MLA_Q_RANK + MLA_KV_RANK:]

    ang = pos_ref[...].astype(F32) * freq_ref[...]
    lane = lax.broadcasted_iota(jnp.int32, ang.shape, 1)
    cos_t = jnp.where(lane < MLA_NOPE + MLA_ROPE, jnp.cos(ang), 0.0)
    sin_t = jnp.sin(ang)

    def rope(tile):
        return tile * cos_t + pltpu.roll(tile, LANES - MLA_ROPE, 1) * sin_t

    nq = _rms(c_q, gq_ref[...]).astype(BF16)
    q = _dot(nq, wq_ref[...])
    nkv = _rms(c_kv, gkv_ref[...]).astype(BF16)
    kn = _dot(nkv, wk_ref[...])
    k_pe = rope(kpe)
    for h in range(MLA_HEADS):
        sl = slice(h * LANES, (h + 1) * LANES)
        qm_ref[:, sl] = (rope(q[:, sl]) * (MLA_SCALE * LOG2E)).astype(BF16)
        km_ref[:, sl] = (kn[:, sl] + k_pe).astype(BF16)
    vm_ref[...] = _dot(nkv, wv_ref[...]).astype(BF16).T

    w = SB_HEADS * SB_DIM
    qs_ref[...] = (sb[:, :w] * (SB_SCALE * LOG2E)).astype(BF16)
    ks_ref[...] = sb[:, w:2 * w].astype(BF16)
    vs_ref[...] = sb[:, 2 * w:].astype(BF16).T


def _const_spec(shape):
    return pl.BlockSpec(shape, lambda *_: (0,) * len(shape), pipeline_mode=pl.Buffered(1))


def _proj_call(x2, pos2, freq, g_mix, wa, wsb, gq, wq, gkv, wk, wv, *, tm):
    T, D = x2.shape
    row = lambda w: pl.BlockSpec((tm, w), lambda i: (i, 0))
    consts = (freq, g_mix, wa, wsb, gq, wq, gkv, wk, wv)
    col = lambda w: pl.BlockSpec((w, tm), lambda i: (0, i))
    w_mla, w_v, w_sb = MLA_HEADS * LANES, MLA_HEADS * MLA_V, SB_HEADS * SB_DIM
    tok = lambda w: jax.ShapeDtypeStruct((T, w), BF16)
    tok_t = lambda w: jax.ShapeDtypeStruct((w, T), BF16)
    return pl.pallas_call(
        _proj_kernel,
        grid=(T // tm,),
        in_specs=[row(D), row(1)] + [_const_spec(c.shape) for c in consts],
        out_specs=[row(w_mla), row(w_mla), col(w_v), row(w_sb), row(w_sb), col(w_sb)],
        out_shape=[tok(w_mla), tok(w_mla), tok_t(w_v), tok(w_sb), tok(w_sb), tok_t(w_sb)],
        compiler_params=pltpu.CompilerParams(dimension_semantics=("parallel",),
                                             vmem_limit_bytes=VMEM_LIMIT),
        name="proj",
    )(x2, pos2, *consts)


ONES_ROWS = 16
MAX_LAG = 64.0


def _mla_kernel(q_ref, k_ref, vt_ref, o_ref, acc_ref, m_ref, qt_ref, p_ref, scale_ref, lag_ref, *, t, hp, wide):
    qi = pl.program_id(2)
    for h in range(hp):
        qt_ref[h] = q_ref[:, h * LANES:(h + 1) * LANES].astype(F32).T.astype(BF16)
    key_l = lax.broadcasted_iota(jnp.int32, (t, t), 0)
    qry_l = lax.broadcasted_iota(jnp.int32, (t, t), 1)

    def scores(j, n_blk):
        k0 = pl.multiple_of(j * t, t)
        return [_dot(k_ref[pl.ds(k0, n_blk * t), h * LANES:(h + 1) * LANES], qt_ref[h]) for h in range(hp)]

    def values(h, j, n_blk):
        k0 = pl.multiple_of(j * t, t)
        w = n_blk * t
        return jnp.concatenate([vt_ref[h * MLA_V:(h + 1) * MLA_V, pl.ds(k0, w)],
                                jnp.ones((ONES_ROWS, w), BF16)], axis=0)

    def one_pass(h, s, diag_last, first):
        if diag_last:
            n_keys = s.shape[0]
            key_g = lax.broadcasted_iota(jnp.int32, (n_keys, t), 0)
            qry_g = lax.broadcasted_iota(jnp.int32, (n_keys, t), 1) + (n_keys - t)
            s = jnp.where(key_g <= qry_g, s, NEG_INF)
        m_old = s[0:1, :] if first else m_ref[h]
        p = jnp.exp2(s - m_old).astype(BF16)
        lag = jnp.maximum(jnp.max(s, axis=0, keepdims=True) - m_old, 0.0)
        m_ref[h] = m_old + lag
        lag_ref[h] = jnp.maximum(lag_ref[h], lag)
        return p, jnp.exp2(-lag)

    def add_pv(h, j, n_blk, p, scale):
        acc_ref[h] = (acc_ref[h] + _dot(values(h, j, n_blk), p)) * scale

    def refill(g, slot, s_all, first):
        for h in range(hp):
            p_ref[slot, h], scale_ref[slot, h] = one_pass(h, s_all[h], False, first)

    def back(g, slot):
        for h in range(hp):
            add_pv(h, g * wide, wide, p_ref[slot, h], scale_ref[slot, h])

    def pipelined(g, slot):
        s_all = scores(g * wide, wide)
        back(g - 1, 1 - slot)
        refill(g, slot, s_all, False)

    def body(i, c):
        pipelined(2 * i + 1, 1)
        pipelined(2 * i + 2, 0)
        return c

    def tail(n_blk, pending_slot):
        j = qi + 1 - n_blk
        s_all = scores(j, n_blk)
        if pending_slot is not None:
            back(n_groups - 1, pending_slot)
        for h in range(hp):
            add_pv(h, j, n_blk, *one_pass(h, s_all[h], True, pending_slot is None))

    acc_ref[...] = jnp.zeros_like(acc_ref)
    lag_ref[...] = jnp.zeros_like(lag_ref)
    n_groups = qi // wide
    n_pairs = jnp.maximum(n_groups - 1, 0) // 2

    @pl.when(n_groups >= 1)
    def _():
        refill(0, 0, scores(0, wide), True)

    lax.fori_loop(0, n_pairs, body, 0)

    @pl.when(jnp.logical_and(n_groups >= 2, n_groups % 2 == 0))
    def _():
        pipelined(n_groups - 1, 1)

    for n_blk in range(1, wide + 1):
        is_size = qi % wide == n_blk - 1
        pl.when(jnp.logical_and(is_size, n_groups == 0))(functools.partial(tail, n_blk, None))
        for slot in (0, 1):
            has_slot = jnp.logical_and(n_groups >= 1, (n_groups - 1) % 2 == slot)
            pl.when(jnp.logical_and(is_size, has_slot))(functools.partial(tail, n_blk, slot))

    def two_pass_step(j, masked):
        s_all = scores(j, 1)
        for h in range(hp):
            s = s_all[h]
            if masked:
                s = jnp.where(key_l <= qry_l, s, NEG_INF)
            m_prev = m_ref[h]
            m_new = jnp.maximum(m_prev, jnp.max(s, axis=0, keepdims=True))
            m_ref[h] = m_new
            acc_ref[h] = jnp.exp2(m_prev - m_new) * acc_ref[h] + _dot(values(h, j, 1), jnp.exp2(s - m_new).astype(BF16))

    @pl.when(jnp.max(lag_ref[...]) > MAX_LAG)
    def _():
        acc_ref[...] = jnp.zeros_like(acc_ref)
        m_ref[...] = jnp.full_like(m_ref, NEG_INF)
        lax.fori_loop(0, qi, lambda j, c: (two_pass_step(j, False), c)[1], 0)
        two_pass_step(qi, True)

    for pair in range(hp // HEAD_PAIR):
        outs = []
        for h in (pair * HEAD_PAIR, pair * HEAD_PAIR + 1):
            a = acc_ref[h]
            outs.append(a[:MLA_V] / a[MLA_V:MLA_V + 1])
        o_t = jnp.concatenate(outs, axis=0)
        o_ref[:, pair * LANES:(pair + 1) * LANES] = o_t.T.astype(o_ref.dtype)


def _mla_call(qm, km, vmt, *, t, hp, wide):
    B, S, _ = qm.shape
    return pl.pallas_call(
        functools.partial(_mla_kernel, t=t, hp=hp, wide=wide),
        grid=(B, MLA_HEADS // hp, S // t),
        in_specs=[pl.BlockSpec((None, t, hp * LANES), lambda b, g, qi: (b, qi, g)),
                  pl.BlockSpec((None, S, hp * LANES), lambda b, g, qi: (b, 0, g)),
                  pl.BlockSpec((hp * MLA_V, S), lambda b, g, qi: (g, b))],
        out_specs=pl.BlockSpec((None, t, hp * MLA_V), lambda b, g, qi: (b, qi, g)),
        out_shape=jax.ShapeDtypeStruct((B, S, MLA_HEADS * MLA_V), BF16),
        scratch_shapes=[pltpu.VMEM((hp, MLA_V + ONES_ROWS, t), F32),
                        pltpu.VMEM((hp, 1, t), F32),
                        pltpu.VMEM((hp, LANES, t), BF16),
                        pltpu.VMEM((2, hp, wide * t, t), BF16),
                        pltpu.VMEM((2, hp, 1, t), F32),
                        pltpu.VMEM((hp, 1, t), F32)],
        compiler_params=pltpu.CompilerParams(dimension_semantics=("parallel", "parallel", "arbitrary"),
                                             vmem_limit_bytes=VMEM_LIMIT),
        name="mla_attn",
    )(qm, km, vmt)


def _sb_kernel(q_ref, k_ref, vt_ref, tri_ref, o_ref, acc_ref, carry_ref, qt_ref, a_ref, tot_ref, *, t, hp, wide):
    qi = pl.program_id(2)
    acc_ref[...] = jnp.zeros_like(acc_ref)
    carry_ref[...] = jnp.zeros_like(carry_ref)
    q_lane = lax.broadcasted_iota(jnp.int32, (t, LANES), 1)
    for pair in range(hp // HEAD_PAIR):
        q = q_ref[:, pair * LANES:(pair + 1) * LANES].astype(F32)
        qt_ref[pair * HEAD_PAIR] = jnp.where(q_lane < SB_DIM, q, 0.0).T.astype(BF16)
        qt_ref[pair * HEAD_PAIR + 1] = jnp.where(q_lane >= SB_DIM, q, 0.0).T.astype(BF16)
    key_l = lax.broadcasted_iota(jnp.int32, (t, t), 0)
    qry_l = lax.broadcasted_iota(jnp.int32, (t, t), 1)

    valid = key_l < qry_l

    def key_offsets(j_top, n_blk):
        return [pl.multiple_of((j_top - b) * t, t) for b in range(n_blk)]

    def scores(j_top, n_blk):
        zs = {}
        for b, k0 in enumerate(key_offsets(j_top, n_blk)):
            for h in range(hp):
                pair = h // HEAD_PAIR
                k_blk = k_ref[pl.ds(k0, t), pair * LANES:(pair + 1) * LANES]
                zs[b, h] = _dot(k_blk, qt_ref[h])
        return zs

    def suffix(z, masked):
        sp = jnp.maximum(z, jnp.log2(1.0 + jnp.exp2(jnp.minimum(z, SOFTPLUS_CLAMP))))
        if masked:
            sp = jnp.where(valid, sp, 0.0)
        c = _dot(tri_ref[...], sp.astype(BF16))
        a = jnp.exp2(z - c)
        if masked:
            a = jnp.where(valid, a, 0.0)
        return a.astype(BF16), c[0:1, :]

    def weights_pv(h, k0, a, total):
        vt = vt_ref[h * SB_DIM:(h + 1) * SB_DIM, pl.ds(k0, t)]
        carry = carry_ref[h]
        acc_ref[h] += _dot(vt, a) * jnp.exp2(carry)
        carry_ref[h] = carry - total

    def step(j_top, n_blk, masked):
        zs = scores(j_top, n_blk)
        mid = {bh: suffix(z, masked) for bh, z in zs.items()}
        for b, k0 in enumerate(key_offsets(j_top, n_blk)):
            for h in range(hp):
                weights_pv(h, k0, *mid[b, h])

    def group_top(g):
        return qi - 2 - g * wide

    def refill(g, zs):
        for (b, h), z in zs.items():
            a_ref[g % 2, b, h], tot_ref[g % 2, b, h] = suffix(z, False)

    def back(g):
        for b, k0 in enumerate(key_offsets(group_top(g), wide)):
            for h in range(hp):
                weights_pv(h, k0, a_ref[g % 2, b, h], tot_ref[g % 2, b, h])

    def alive():
        return (jnp.max(carry_ref[...]) > EXP2_IS_ZERO_BELOW).astype(jnp.int32)

    step(qi, 1, True)

    @pl.when(qi >= 1)
    def _():
        step(qi - 1, 1, False)

    n_rest = jnp.maximum(qi - 1, 0)
    n_wide = n_rest // wide

    @pl.when(jnp.logical_and(n_wide >= 1, alive() == 1))
    def _():
        refill(0, scores(group_top(0), wide))

    def cond(state):
        g, live = state
        return jnp.logical_and(g < n_wide, live == 1)

    def body(state):
        g, _ = state
        zs = scores(group_top(g), wide)
        back(g - 1)
        refill(g, zs)
        return g + 1, alive()

    g_end, live = lax.while_loop(cond, body, (jnp.int32(1), alive()))

    @pl.when(jnp.logical_and(n_wide >= 1, live == 1))
    def _():
        back(g_end - 1)

    for r in range(wide - 1):
        @pl.when(jnp.logical_and(n_rest - n_wide * wide > r, alive() == 1))
        def _():
            step(qi - 2 - n_wide * wide - r, 1, False)

    for pair in range(hp // HEAD_PAIR):
        o_t = jnp.concatenate([acc_ref[pair * HEAD_PAIR], acc_ref[pair * HEAD_PAIR + 1]], axis=0)
        o_ref[:, pair * LANES:(pair + 1) * LANES] = o_t.T.astype(o_ref.dtype)


def _sb_call(qs, ks, vst, *, t, hp, wide):
    B, S, _ = qs.shape
    r = lax.broadcasted_iota(jnp.int32, (t, t), 0)
    c = lax.broadcasted_iota(jnp.int32, (t, t), 1)
    tri = (c >= r).astype(BF16)
    w = hp * SB_DIM
    return pl.pallas_call(
        functools.partial(_sb_kernel, t=t, hp=hp, wide=wide),
        grid=(B, SB_HEADS // hp, S // t),
        in_specs=[pl.BlockSpec((None, t, w), lambda b, g, qi: (b, qi, g)),
                  pl.BlockSpec((None, S, w), lambda b, g, qi: (b, 0, g)),
                  pl.BlockSpec((w, S), lambda b, g, qi: (g, b)),
                  pl.BlockSpec((t, t), lambda b, g, qi: (0, 0))],
        out_specs=pl.BlockSpec((None, t, w), lambda b, g, qi: (b, qi, g)),
        out_shape=jax.ShapeDtypeStruct((B, S, SB_HEADS * SB_DIM), BF16),
        scratch_shapes=[pltpu.VMEM((hp, SB_DIM, t), F32),
                        pltpu.VMEM((hp, 1, t), F32),
                        pltpu.VMEM((hp, LANES, t), BF16),
                        pltpu.VMEM((2, wide, hp, t, t), BF16),
                        pltpu.VMEM((2, wide, hp, 1, t), F32)],
        compiler_params=pltpu.CompilerParams(dimension_semantics=("parallel", "parallel", "arbitrary"),
                                             vmem_limit_bytes=VMEM_LIMIT),
        name="sb_attn",
    )(qs, ks, vst, tri)


def _post_kernel(x_ref, oa_ref, ob_ref, p_ref, g_mix_ref, wga_ref, wgb_ref, wbra_ref, wbrb_ref, wout_ref,
                 g_ffn_ref, wfg_ref, wfu_ref, wfd_ref, wpg_ref, wpp_ref, g_ple_ref, g_fin_ref, out_ref,
                 *, ff_chunk, final_norm):
    x = x_ref[...]
    n = _rms(x, g_mix_ref[...]).astype(BF16)
    merged = (jax.nn.sigmoid(_dot(n, wga_ref[...])) * _dot(oa_ref[...], wbra_ref[...])
              + jax.nn.sigmoid(_dot(n, wgb_ref[...])) * _dot(ob_ref[...], wbrb_ref[...]))
    h = x + _dot(merged.astype(BF16), wout_ref[...])

    n2 = _rms(h, g_ffn_ref[...]).astype(BF16)
    d_ff = wfg_ref.shape[1]
    ff = None
    for c0 in range(0, d_ff, ff_chunk):
        c1 = min(c0 + ff_chunk, d_ff)
        g = _dot(n2, wfg_ref[:, c0:c1])
        u = _dot(n2, wfu_ref[:, c0:c1])
        part = _dot((g * jax.nn.sigmoid(g) * u).astype(BF16), wfd_ref[c0:c1, :])
        ff = part if ff is None else ff + part
    h = h + ff

    e = _rms(_dot(p_ref[...].astype(BF16), wpp_ref[...]), g_ple_ref[...])
    h = h + jax.nn.sigmoid(_dot(h.astype(BF16), wpg_ref[...])) * e
    out_ref[...] = _rms(h, g_fin_ref[...]) if final_norm else h


def _post_call(x2, oa, ob, p2, consts, *, tm, ff_chunk, final_norm):
    T, D = x2.shape
    row = lambda w: pl.BlockSpec((tm, w), lambda i: (i, 0))
    return pl.pallas_call(
        functools.partial(_post_kernel, ff_chunk=ff_chunk, final_norm=final_norm),
        grid=(T // tm,),
        in_specs=[row(D), row(oa.shape[1]), row(ob.shape[1]), row(p2.shape[1])]
                 + [_const_spec(c.shape) for c in consts],
        out_specs=row(D),
        out_shape=jax.ShapeDtypeStruct((T, D), F32),
        compiler_params=pltpu.CompilerParams(dimension_semantics=("parallel",),
                                             vmem_limit_bytes=VMEM_LIMIT),
        name="post",
    )(x2, oa, ob, p2, *consts)


def _rotate_half_cols(w):
    half = w.shape[-1] // 2
    return jnp.concatenate([-w[..., half:], w[..., :half]], axis=-1)


def _layer_weights(w_in, w_q_b, w_kv_b):
    d = w_in.shape[0]
    o = 0
    cols = []
    for wd in (MLA_Q_RANK, MLA_KV_RANK, MLA_ROPE, SB_HEADS * SB_DIM, SB_HEADS * SB_DIM, SB_HEADS * SB_DIM, d, d):
        cols.append(w_in[:, o:o + wd])
        o += wd
    w_cq, w_ckv, w_kpe, w_qs, w_ks, w_vs, w_ga, w_gb = cols

    kpe_tile = jnp.concatenate([jnp.zeros_like(w_in[:, :MLA_NOPE]), w_kpe, _rotate_half_cols(w_kpe)], axis=1)
    wa = jnp.concatenate([w_cq, w_ckv, kpe_tile], axis=1)
    wsb = jnp.concatenate([w_qs, w_ks, w_vs], axis=1)

    qb = w_q_b.reshape(MLA_Q_RANK, MLA_HEADS, MLA_NOPE + MLA_ROPE)
    wq = jnp.concatenate([qb, _rotate_half_cols(qb[..., MLA_NOPE:])], axis=-1).reshape(MLA_Q_RANK, MLA_HEADS * LANES)

    kvb = w_kv_b.reshape(MLA_KV_RANK, MLA_HEADS, MLA_NOPE + MLA_V)
    wk = jnp.pad(kvb[..., :MLA_NOPE], ((0, 0), (0, 0), (0, LANES - MLA_NOPE))).reshape(MLA_KV_RANK, MLA_HEADS * LANES)
    wv = kvb[..., MLA_NOPE:].reshape(MLA_KV_RANK, MLA_HEADS * MLA_V)
    bf = lambda t: t.astype(BF16)
    return tuple(map(bf, (wa, wsb, wq, wk, wv, w_ga, w_gb)))


def _rope_freq_row():
    inv_freq = 1.0 / (ROPE_THETA ** (jnp.arange(0, MLA_ROPE, 2, dtype=F32) / MLA_ROPE))
    both = jnp.concatenate([inv_freq, inv_freq])
    return jnp.pad(both, (MLA_NOPE, LANES - MLA_NOPE - MLA_ROPE)).reshape(1, LANES)


def _tiles(B, S):
    T = B * S
    tm_proj = min(512, T)
    tm_post = min(512, T)
    t_attn = min(256, S)
    heads_per_step = 8
    wide = 2
    return tm_proj, tm_post, t_attn, heads_per_step, wide


def kernel(x, p, positions, g_mix, w_in, g_q_a, w_q_b, g_kv_a, w_kv_b, w_br_mla, w_br_sb, w_out, g_ffn,
           w_ffn_gate, w_ffn_up, w_ffn_down, w_ple_gate, w_ple_proj, g_ple, g_final):
    B, S, D = x.shape
    T = B * S
    depth = w_in.shape[0]
    tm_proj, tm_post, t_attn, hp, wide = _tiles(B, S)
    d_ff = w_ffn_gate.shape[-1]
    ff_chunk = min(d_ff, 1024)
    row = lambda g: g.reshape(1, -1).astype(F32)
    bf = lambda t: t.astype(BF16)

    pos2 = positions.reshape(T, 1).astype(jnp.int32)
    freq = _rope_freq_row()
    h = x.reshape(T, D)
    for i in range(depth):
        wa, wsb, wq, wk, wv, w_ga, w_gb = _layer_weights(w_in[i], w_q_b[i], w_kv_b[i])
        qm, km, vm, qs, ks, vs = _proj_call(h, pos2, freq, row(g_mix[i]), wa, wsb, row(g_q_a[i]), wq,
                                            row(g_kv_a[i]), wk, wv, tm=tm_proj)
        shp = lambda t: t.reshape(B, S, t.shape[-1])
        o_a = _mla_call(shp(qm), shp(km), vm, t=t_attn, hp=hp, wide=wide).reshape(T, -1)
        o_b = _sb_call(shp(qs), shp(ks), vs, t=t_attn, hp=hp, wide=wide).reshape(T, -1)
        consts = (row(g_mix[i]), w_ga, w_gb, bf(w_br_mla[i]), bf(w_br_sb[i]), bf(w_out[i]), row(g_ffn[i]),
                  bf(w_ffn_gate[i]), bf(w_ffn_up[i]), bf(w_ffn_down[i]), bf(w_ple_gate[i]), bf(w_ple_proj[i]),
                  row(g_ple[i]), row(g_final))
        h = _post_call(h, o_a, o_b, p[i].reshape(T, -1), consts, tm=tm_post, ff_chunk=ff_chunk,
                       final_norm=(i == depth - 1))
    return h.reshape(B, S, D)
```

```python
import functools
import math

import jax
import jax.numpy as jnp
from jax import lax
from jax.experimental import pallas as pl
from jax.experimental.pallas import tpu as pltpu

EPS = 1e-6
MLA_HEADS = 8
MLA_NOPE = 64
MLA_ROPE = 32
MLA_V = 64
MLA_Q_RANK = 384
MLA_KV_RANK = 256
ROPE_THETA = 10000.0
MLA_SCALE = 1.0 / math.sqrt(MLA_NOPE + MLA_ROPE)
SB_HEADS = 8
SB_DIM = 64
SB_SCALE = 1.0 / math.sqrt(SB_DIM)
NEG_INF = -1e30
LOG2E = math.log2(math.e)
SOFTPLUS_CLAMP = 64.0
EXP2_IS_ZERO_BELOW = -151.0

LANES = 128
HEAD_PAIR = 2
VMEM_LIMIT = 56 * 1024 * 1024

F32 = jnp.float32
BF16 = jnp.bfloat16


def _rms(x, g):
    return x * lax.rsqrt(jnp.mean(x * x, axis=-1, keepdims=True) + EPS) * g


def _dot(a, b):
    return jnp.dot(a, b, preferred_element_type=F32)


def _dot_nt(a, b):
    return lax.dot_general(a, b, (((1,), (1,)), ((), ())), preferred_element_type=F32)


def _proj_kernel(x_ref, pos_ref, freq_ref, g_mix_ref, wa_ref, wsb_ref, gq_ref, wq_ref,
                 gkv_ref, wk_ref, wv_ref,
                 qm_ref, km_ref, vm_ref, qs_ref, ks_ref, vs_ref):
    n = _rms(x_ref[...], g_mix_ref[...]).astype(BF16)
    pa = _dot(n, wa_ref[...])
    sb = _dot(n, wsb_ref[...])
    c_q = pa[:, :MLA_Q_RANK]
    c_kv = pa[:, MLA_Q_RANK:MLA_Q_RANK + MLA_KV_RANK]
    kpe = pa[:, MLA_Q_RANK + MLA_KV_RANK:]

    ang = pos_ref[...].astype(F32) * freq_ref[...]
    lane = lax.broadcasted_iota(jnp.int32, ang.shape, 1)
    cos_t = jnp.where(lane < MLA_NOPE + MLA_ROPE, jnp.cos(ang), 0.0)
    sin_t = jnp.sin(ang)

    def rope(tile):
        return tile * cos_t + pltpu.roll(tile, LANES - MLA_ROPE, 1) * sin_t

    nq = _rms(c_q, gq_ref[...]).astype(BF16)
    q = _dot(nq, wq_ref[...])
    nkv = _rms(c_kv, gkv_ref[...]).astype(BF16)
    kn = _dot(nkv, wk_ref[...])
    k_pe = rope(kpe)
    for h in range(MLA_HEADS):
        sl = slice(h * LANES, (h + 1) * LANES)
        qm_ref[:, sl] = (rope(q[:, sl]) * (MLA_SCALE * LOG2E)).astype(BF16)
        km_ref[:, sl] = (kn[:, sl] + k_pe).astype(BF16)
    vm_ref[...] = _dot(nkv, wv_ref[...]).astype(BF16).T

    w = SB_HEADS * SB_DIM
    qs_ref[...] = (sb[:, :w] * (SB_SCALE * LOG2E)).astype(BF16)
    ks_ref[...] = sb[:, w:2 * w].astype(BF16)
    vs_ref[...] = sb[:, 2 * w:].astype(BF16).T


def _const_spec(shape):
    return pl.BlockSpec(shape, lambda *_: (0,) * len(shape), pipeline_mode=pl.Buffered(1))


def _proj_call(x2, pos2, freq, g_mix, wa, wsb, gq, wq, gkv, wk, wv, *, tm):
    T, D = x2.shape
    row = lambda w: pl.BlockSpec((tm, w), lambda i: (i, 0))
    consts = (freq, g_mix, wa, wsb, gq, wq, gkv, wk, wv)
    col = lambda w: pl.BlockSpec((w, tm), lambda i: (0, i))
    w_mla, w_v, w_sb = MLA_HEADS * LANES, MLA_HEADS * MLA_V, SB_HEADS * SB_DIM
    tok = lambda w: jax.ShapeDtypeStruct((T, w), BF16)
    tok_t = lambda w: jax.ShapeDtypeStruct((w, T), BF16)
    return pl.pallas_call(
        _proj_kernel,
        grid=(T // tm,),
        in_specs=[row(D), row(1)] + [_const_spec(c.shape) for c in consts],
        out_specs=[row(w_mla), row(w_mla), col(w_v), row(w_sb), row(w_sb), col(w_sb)],
        out_shape=[tok(w_mla), tok(w_mla), tok_t(w_v), tok(w_sb), tok(w_sb), tok_t(w_sb)],
        compiler_params=pltpu.CompilerParams(dimension_semantics=("parallel",),
                                             vmem_limit_bytes=VMEM_LIMIT),
        name="proj",
    )(x2, pos2, *consts)


ONES_ROWS = 16
MAX_LAG = 64.0


def _mla_kernel(q_ref, k_ref, vt_ref, o_ref, acc_ref, m_ref, qt_ref, p_ref, scale_ref, lag_ref, *, t, hp, wide):
    qi = pl.program_id(2)
    for h in range(hp):
        qt_ref[h] = q_ref[:, h * LANES:(h + 1) * LANES].astype(F32).T.astype(BF16)
    key_l = lax.broadcasted_iota(jnp.int32, (t, t), 0)
    qry_l = lax.broadcasted_iota(jnp.int32, (t, t), 1)

    def scores(j, n_blk):
        k0 = pl.multiple_of(j * t, t)
        return [_dot(k_ref[pl.ds(k0, n_blk * t), h * LANES:(h + 1) * LANES], qt_ref[h]) for h in range(hp)]

    def values(h, j, n_blk):
        k0 = pl.multiple_of(j * t, t)
        w = n_blk * t
        return jnp.concatenate([vt_ref[h * MLA_V:(h + 1) * MLA_V, pl.ds(k0, w)],
                                jnp.ones((ONES_ROWS, w), BF16)], axis=0)

    def one_pass(h, s, diag_last, first):
        if diag_last:
            n_keys = s.shape[0]
            key_g = lax.broadcasted_iota(jnp.int32, (n_keys, t), 0)
            qry_g = lax.broadcasted_iota(jnp.int32, (n_keys, t), 1) + (n_keys - t)
            s = jnp.where(key_g <= qry_g, s, NEG_INF)
        m_old = s[0:1, :] if first else m_ref[h]
        p = jnp.exp2(s - m_old).astype(BF16)
        lag = jnp.maximum(jnp.max(s, axis=0, keepdims=True) - m_old, 0.0)
        m_ref[h] = m_old + lag
        lag_ref[h] = jnp.maximum(lag_ref[h], lag)
        return p, jnp.exp2(-lag)

    def add_pv(h, j, n_blk, p, scale):
        acc_ref[h] = (acc_ref[h] + _dot(values(h, j, n_blk), p)) * scale

    def refill(g, slot, s_all, first):
        for h in range(hp):
            p_ref[slot, h], scale_ref[slot, h] = one_pass(h, s_all[h], False, first)

    def back(g, slot):
        for h in range(hp):
            add_pv(h, g * wide, wide, p_ref[slot, h], scale_ref[slot, h])

    def pipelined(g, slot):
        s_all = scores(g * wide, wide)
        back(g - 1, 1 - slot)
        refill(g, slot, s_all, False)

    def body(i, c):
        pipelined(2 * i + 1, 1)
        pipelined(2 * i + 2, 0)
        return c

    def tail(n_blk, pending_slot):
        j = qi + 1 - n_blk
        s_all = scores(j, n_blk)
        if pending_slot is not None:
            back(n_groups - 1, pending_slot)
        for h in range(hp):
            add_pv(h, j, n_blk, *one_pass(h, s_all[h], True, pending_slot is None))

    acc_ref[...] = jnp.zeros_like(acc_ref)
    lag_ref[...] = jnp.zeros_like(lag_ref)
    n_groups = qi // wide
    n_pairs = jnp.maximum(n_groups - 1, 0) // 2

    @pl.when(n_groups >= 1)
    def _():
        refill(0, 0, scores(0, wide), True)

    lax.fori_loop(0, n_pairs, body, 0)

    @pl.when(jnp.logical_and(n_groups >= 2, n_groups % 2 == 0))
    def _():
        pipelined(n_groups - 1, 1)

    for n_blk in range(1, wide + 1):
        is_size = qi % wide == n_blk - 1
        pl.when(jnp.logical_and(is_size, n_groups == 0))(functools.partial(tail, n_blk, None))
        for slot in (0, 1):
            has_slot = jnp.logical_and(n_groups >= 1, (n_groups - 1) % 2 == slot)
            pl.when(jnp.logical_and(is_size, has_slot))(functools.partial(tail, n_blk, slot))

    def two_pass_step(j, masked):
        s_all = scores(j, 1)
        for h in range(hp):
            s = s_all[h]
            if masked:
                s = jnp.where(key_l <= qry_l, s, NEG_INF)
            m_prev = m_ref[h]
            m_new = jnp.maximum(m_prev, jnp.max(s, axis=0, keepdims=True))
            m_ref[h] = m_new
            acc_ref[h] = jnp.exp2(m_prev - m_new) * acc_ref[h] + _dot(values(h, j, 1), jnp.exp2(s - m_new).astype(BF16))

    @pl.when(jnp.max(lag_ref[...]) > MAX_LAG)
    def _():
        acc_ref[...] = jnp.zeros_like(acc_ref)
        m_ref[...] = jnp.full_like(m_ref, NEG_INF)
        lax.fori_loop(0, qi, lambda j, c: (two_pass_step(j, False), c)[1], 0)
        two_pass_step(qi, True)

    for pair in range(hp // HEAD_PAIR):
        outs = []
        for h in (pair * HEAD_PAIR, pair * HEAD_PAIR + 1):
            a = acc_ref[h]
            outs.append(a[:MLA_V] / a[MLA_V:MLA_V + 1])
        o_t = jnp.concatenate(outs, axis=0)
        o_ref[:, pair * LANES:(pair + 1) * LANES] = o_t.T.astype(o_ref.dtype)


def _mla_call(qm, km, vmt, *, t, hp, wide):
    B, S, _ = qm.shape
    return pl.pallas_call(
        functools.partial(_mla_kernel, t=t, hp=hp, wide=wide),
        grid=(B, MLA_HEADS // hp, S // t),
        in_specs=[pl.BlockSpec((None, t, hp * LANES), lambda b, g, qi: (b, qi, g)),
                  pl.BlockSpec((None, S, hp * LANES), lambda b, g, qi: (b, 0, g)),
                  pl.BlockSpec((hp * MLA_V, S), lambda b, g, qi: (g, b))],
        out_specs=pl.BlockSpec((None, t, hp * MLA_V), lambda b, g, qi: (b, qi, g)),
        out_shape=jax.ShapeDtypeStruct((B, S, MLA_HEADS * MLA_V), BF16),
        scratch_shapes=[pltpu.VMEM((hp, MLA_V + ONES_ROWS, t), F32),
                        pltpu.VMEM((hp, 1, t), F32),
                        pltpu.VMEM((hp, LANES, t), BF16),
                        pltpu.VMEM((2, hp, wide * t, t), BF16),
                        pltpu.VMEM((2, hp, 1, t), F32),
                        pltpu.VMEM((hp, 1, t), F32)],
        compiler_params=pltpu.CompilerParams(dimension_semantics=("parallel", "parallel", "arbitrary"),
                                             vmem_limit_bytes=VMEM_LIMIT),
        name="mla_attn",
    )(qm, km, vmt)


def _sb_kernel(q_ref, k_ref, vt_ref, tri_ref, o_ref, acc_ref, carry_ref, qt_ref, a_ref, tot_ref,
               *, t, hp, wide, tiles):
    step_i = pl.program_id(2)
    acc_ref[...] = jnp.zeros_like(acc_ref)
    carry_ref[...] = jnp.zeros_like(carry_ref)
    q_lane = lax.broadcasted_iota(jnp.int32, (t, LANES), 1)
    for tile in range(tiles):
        for pair in range(hp // HEAD_PAIR):
            q = q_ref[tile * t:(tile + 1) * t, pair * LANES:(pair + 1) * LANES].astype(F32)
            qt_ref[tile, pair * HEAD_PAIR] = jnp.where(q_lane < SB_DIM, q, 0.0).T.astype(BF16)
            qt_ref[tile, pair * HEAD_PAIR + 1] = jnp.where(q_lane >= SB_DIM, q, 0.0).T.astype(BF16)
    key_l = lax.broadcasted_iota(jnp.int32, (t, t), 0)
    qry_l = lax.broadcasted_iota(jnp.int32, (t, t), 1)

    valid = key_l < qry_l

    def tile_qi(tile):
        return step_i * tiles + tile

    def key_offset(j):
        return pl.multiple_of(j * t, t)

    def scores(items):
        zs = {}
        for n, (tile, j) in enumerate(items):
            for h in range(hp):
                pair = h // HEAD_PAIR
                k_blk = k_ref[pl.ds(key_offset(j), t), pair * LANES:(pair + 1) * LANES]
                zs[n, h] = _dot(k_blk, qt_ref[tile, h])
        return zs

    def suffix(z, masked):
        sp = jnp.maximum(z, jnp.log2(1.0 + jnp.exp2(jnp.minimum(z, SOFTPLUS_CLAMP))))
        if masked:
            sp = jnp.where(valid, sp, 0.0)
        c = _dot(tri_ref[...], sp.astype(BF16))
        a = jnp.exp2(z - c)
        if masked:
            a = jnp.where(valid, a, 0.0)
        return a.astype(BF16), c[0:1, :]

    def weights_pv(tile, h, j, a, total):
        vt = vt_ref[h * SB_DIM:(h + 1) * SB_DIM, pl.ds(key_offset(j), t)]
        carry = carry_ref[tile, h]
        acc_ref[tile, h] += _dot(vt, a) * jnp.exp2(carry)
        carry_ref[tile, h] = carry - total

    def step(items, masked):
        zs = scores(items)
        mid = {nh: suffix(z, masked) for nh, z in zs.items()}
        for n, (tile, j) in enumerate(items):
            for h in range(hp):
                weights_pv(tile, h, j, *mid[n, h])

    step([(tile, tile_qi(tile)) for tile in range(tiles)], True)

    @pl.when(step_i >= 1)
    def _():
        step([(tile, tile_qi(tile) - 1) for tile in range(tiles)], False)

    if tiles > 1:
        @pl.when(step_i == 0)
        def _():
            step([(tile, tile_qi(tile) - 1) for tile in range(1, tiles)], False)

    def rest_of_tile(tile, c):
        qi = tile_qi(tile)

        def group_items(g):
            return [(tile, qi - 2 - g * wide - b) for b in range(wide)]

        def refill(g, zs):
            for (b, h), z in zs.items():
                a_ref[g % 2, b, h], tot_ref[g % 2, b, h] = suffix(z, False)

        def back(g):
            for b, (_, j) in enumerate(group_items(g)):
                for h in range(hp):
                    weights_pv(tile, h, j, a_ref[g % 2, b, h], tot_ref[g % 2, b, h])

        def alive():
            return (jnp.max(carry_ref[tile]) > EXP2_IS_ZERO_BELOW).astype(jnp.int32)

        n_rest = jnp.maximum(qi - 1, 0)
        n_wide = n_rest // wide
        live_0 = alive()

        @pl.when(jnp.logical_and(n_wide >= 1, live_0 == 1))
        def _():
            refill(0, scores(group_items(0)))

        def cond(state):
            g, live = state
            return jnp.logical_and(g < n_wide, live == 1)

        def body(state):
            g, _ = state
            zs = scores(group_items(g))
            back(g - 1)
            refill(g, zs)
            return g + 1, alive()

        g_end, live = lax.while_loop(cond, body, (jnp.int32(1), live_0))

        @pl.when(jnp.logical_and(n_wide >= 1, live == 1))
        def _():
            back(g_end - 1)

        for r in range(wide - 1):
            @pl.when(jnp.logical_and(n_rest - n_wide * wide > r, live == 1))
            def _():
                step([(tile, qi - 2 - n_wide * wide - r)], False)
        return c

    lax.fori_loop(0, tiles, rest_of_tile, 0)

    for tile in range(tiles):
        for pair in range(hp // HEAD_PAIR):
            o_t = jnp.concatenate([acc_ref[tile, pair * HEAD_PAIR], acc_ref[tile, pair * HEAD_PAIR + 1]], axis=0)
            o_ref[tile * t:(tile + 1) * t, pair * LANES:(pair + 1) * LANES] = o_t.T.astype(o_ref.dtype)


def _sb_call(qs, ks, vst, *, t, hp, wide, tiles):
    B, S, _ = qs.shape
    r = lax.broadcasted_iota(jnp.int32, (t, t), 0)
    c = lax.broadcasted_iota(jnp.int32, (t, t), 1)
    tri = (c >= r).astype(BF16)
    w = hp * SB_DIM
    return pl.pallas_call(
        functools.partial(_sb_kernel, t=t, hp=hp, wide=wide, tiles=tiles),
        grid=(B, SB_HEADS // hp, S // (tiles * t)),
        in_specs=[pl.BlockSpec((None, tiles * t, w), lambda b, g, i: (b, i, g)),
                  pl.BlockSpec((None, S, w), lambda b, g, i: (b, 0, g)),
                  pl.BlockSpec((w, S), lambda b, g, i: (g, b)),
                  pl.BlockSpec((t, t), lambda b, g, i: (0, 0))],
        out_specs=pl.BlockSpec((None, tiles * t, w), lambda b, g, i: (b, i, g)),
        out_shape=jax.ShapeDtypeStruct((B, S, SB_HEADS * SB_DIM), BF16),
        scratch_shapes=[pltpu.VMEM((tiles, hp, SB_DIM, t), F32),
                        pltpu.VMEM((tiles, hp, 1, t), F32),
                        pltpu.VMEM((tiles, hp, LANES, t), BF16),
                        pltpu.VMEM((2, wide, hp, t, t), BF16),
                        pltpu.VMEM((2, wide, hp, 1, t), F32)],
        compiler_params=pltpu.CompilerParams(dimension_semantics=("parallel", "parallel", "arbitrary"),
                                             vmem_limit_bytes=VMEM_LIMIT),
        name="sb_attn",
    )(qs, ks, vst, tri)


def _post_kernel(x_ref, oa_ref, ob_ref, p_ref, g_mix_ref, wga_ref, wgb_ref, wbra_ref, wbrb_ref, wout_ref,
                 g_ffn_ref, wfg_ref, wfu_ref, wfd_ref, wpg_ref, wpp_ref, g_ple_ref, g_fin_ref, out_ref,
                 *, ff_chunk, final_norm):
    x = x_ref[...]
    n = _rms(x, g_mix_ref[...]).astype(BF16)
    merged = (jax.nn.sigmoid(_dot(n, wga_ref[...])) * _dot(oa_ref[...], wbra_ref[...])
              + jax.nn.sigmoid(_dot(n, wgb_ref[...])) * _dot(ob_ref[...], wbrb_ref[...]))
    h = x + _dot(merged.astype(BF16), wout_ref[...])

    n2 = _rms(h, g_ffn_ref[...]).astype(BF16)
    d_ff = wfg_ref.shape[1]
    ff = None
    for c0 in range(0, d_ff, ff_chunk):
        c1 = min(c0 + ff_chunk, d_ff)
        g = _dot(n2, wfg_ref[:, c0:c1])
        u = _dot(n2, wfu_ref[:, c0:c1])
        part = _dot((g * jax.nn.sigmoid(g) * u).astype(BF16), wfd_ref[c0:c1, :])
        ff = part if ff is None else ff + part
    h = h + ff

    e = _rms(_dot(p_ref[...].astype(BF16), wpp_ref[...]), g_ple_ref[...])
    h = h + jax.nn.sigmoid(_dot(h.astype(BF16), wpg_ref[...])) * e
    out_ref[...] = _rms(h, g_fin_ref[...]) if final_norm else h


def _post_call(x2, oa, ob, p2, consts, *, tm, ff_chunk, final_norm):
    T, D = x2.shape
    row = lambda w: pl.BlockSpec((tm, w), lambda i: (i, 0))
    return pl.pallas_call(
        functools.partial(_post_kernel, ff_chunk=ff_chunk, final_norm=final_norm),
        grid=(T // tm,),
        in_specs=[row(D), row(oa.shape[1]), row(ob.shape[1]), row(p2.shape[1])]
                 + [_const_spec(c.shape) for c in consts],
        out_specs=row(D),
        out_shape=jax.ShapeDtypeStruct((T, D), F32),
        compiler_params=pltpu.CompilerParams(dimension_semantics=("parallel",),
                                             vmem_limit_bytes=VMEM_LIMIT),
        name="post",
    )(x2, oa, ob, p2, *consts)


def _rotate_half_cols(w):
    half = w.shape[-1] // 2
    return jnp.concatenate([-w[..., half:], w[..., :half]], axis=-1)


def _layer_weights(w_in, w_q_b, w_kv_b):
    d = w_in.shape[0]
    o = 0
    cols = []
    for wd in (MLA_Q_RANK, MLA_KV_RANK, MLA_ROPE, SB_HEADS * SB_DIM, SB_HEADS * SB_DIM, SB_HEADS * SB_DIM, d, d):
        cols.append(w_in[:, o:o + wd])
        o += wd
    w_cq, w_ckv, w_kpe, w_qs, w_ks, w_vs, w_ga, w_gb = cols

    kpe_tile = jnp.concatenate([jnp.zeros_like(w_in[:, :MLA_NOPE]), w_kpe, _rotate_half_cols(w_kpe)], axis=1)
    wa = jnp.concatenate([w_cq, w_ckv, kpe_tile], axis=1)
    wsb = jnp.concatenate([w_qs, w_ks, w_vs], axis=1)

    qb = w_q_b.reshape(MLA_Q_RANK, MLA_HEADS, MLA_NOPE + MLA_ROPE)
    wq = jnp.concatenate([qb, _rotate_half_cols(qb[..., MLA_NOPE:])], axis=-1).reshape(MLA_Q_RANK, MLA_HEADS * LANES)

    kvb = w_kv_b.reshape(MLA_KV_RANK, MLA_HEADS, MLA_NOPE + MLA_V)
    wk = jnp.pad(kvb[..., :MLA_NOPE], ((0, 0), (0, 0), (0, LANES - MLA_NOPE))).reshape(MLA_KV_RANK, MLA_HEADS * LANES)
    wv = kvb[..., MLA_NOPE:].reshape(MLA_KV_RANK, MLA_HEADS * MLA_V)
    bf = lambda t: t.astype(BF16)
    return tuple(map(bf, (wa, wsb, wq, wk, wv, w_ga, w_gb)))


def _rope_freq_row():
    inv_freq = 1.0 / (ROPE_THETA ** (jnp.arange(0, MLA_ROPE, 2, dtype=F32) / MLA_ROPE))
    both = jnp.concatenate([inv_freq, inv_freq])
    return jnp.pad(both, (MLA_NOPE, LANES - MLA_NOPE - MLA_ROPE)).reshape(1, LANES)


def _tiles(B, S):
    T = B * S
    tm_proj = min(512, T)
    tm_post = min(512, T)
    t_attn = min(256, S)
    heads_per_step = 8
    wide = 2
    sb_tiles = 2
    return tm_proj, tm_post, t_attn, heads_per_step, wide, sb_tiles


def kernel(x, p, positions, g_mix, w_in, g_q_a, w_q_b, g_kv_a, w_kv_b, w_br_mla, w_br_sb, w_out, g_ffn,
           w_ffn_gate, w_ffn_up, w_ffn_down, w_ple_gate, w_ple_proj, g_ple, g_final):
    B, S, D = x.shape
    T = B * S
    depth = w_in.shape[0]
    tm_proj, tm_post, t_attn, hp, wide, sb_tiles = _tiles(B, S)
    d_ff = w_ffn_gate.shape[-1]
    ff_chunk = min(d_ff, 1024)
    row = lambda g: g.reshape(1, -1).astype(F32)
    bf = lambda t: t.astype(BF16)

    pos2 = positions.reshape(T, 1).astype(jnp.int32)
    freq = _rope_freq_row()
    h = x.reshape(T, D)
    for i in range(depth):
        wa, wsb, wq, wk, wv, w_ga, w_gb = _layer_weights(w_in[i], w_q_b[i], w_kv_b[i])
        qm, km, vm, qs, ks, vs = _proj_call(h, pos2, freq, row(g_mix[i]), wa, wsb, row(g_q_a[i]), wq,
                                            row(g_kv_a[i]), wk, wv, tm=tm_proj)
        shp = lambda t: t.reshape(B, S, t.shape[-1])
        o_a = _mla_call(shp(qm), shp(km), vm, t=t_attn, hp=hp, wide=wide).reshape(T, -1)
        o_b = _sb_call(shp(qs), shp(ks), vs, t=t_attn, hp=hp, wide=wide,
                       tiles=sb_tiles if (S // t_attn) % sb_tiles == 0 else 1).reshape(T, -1)
        consts = (row(g_mix[i]), w_ga, w_gb, bf(w_br_mla[i]), bf(w_br_sb[i]), bf(w_out[i]), row(g_ffn[i]),
                  bf(w_ffn_gate[i]), bf(w_ffn_up[i]), bf(w_ffn_down[i]), bf(w_ple_gate[i]), bf(w_ple_proj[i]),
                  row(g_ple[i]), row(g_final))
        h = _post_call(h, o_a, o_b, p[i].reshape(T, -1), consts, tm=tm_post, ff_chunk=ff_chunk,
                       final_norm=(i == depth - 1))
    return h.reshape(B, S, D)
```

```python
import functools
import math

import jax
import jax.numpy as jnp
from jax import lax
from jax.experimental import pallas as pl
from jax.experimental.pallas import tpu as pltpu

EPS = 1e-6
MLA_HEADS = 8
MLA_NOPE = 64
MLA_ROPE = 32
MLA_V = 64
MLA_Q_RANK = 384
MLA_KV_RANK = 256
ROPE_THETA = 10000.0
MLA_SCALE = 1.0 / math.sqrt(MLA_NOPE + MLA_ROPE)
SB_HEADS = 8
SB_DIM = 64
SB_SCALE = 1.0 / math.sqrt(SB_DIM)
NEG_INF = -1e30
LOG2E = math.log2(math.e)
SOFTPLUS_CLAMP = 64.0
EXP2_IS_ZERO_BELOW = -151.0

LANES = 128
HEAD_PAIR = 2
VMEM_LIMIT = 56 * 1024 * 1024

F32 = jnp.float32
BF16 = jnp.bfloat16


def _rms(x, g):
    return x * lax.rsqrt(jnp.mean(x * x, axis=-1, keepdims=True) + EPS) * g


def _dot(a, b):
    return jnp.dot(a, b, preferred_element_type=F32)


def _dot_nt(a, b):
    return lax.dot_general(a, b, (((1,), (1,)), ((), ())), preferred_element_type=F32)


def _proj_kernel(x_ref, pos_ref, freq_ref, g_mix_ref, wa_ref, wsb_ref, gq_ref, wq_ref,
                 gkv_ref, wk_ref, wv_ref,
                 qm_ref, km_ref, vm_ref, qs_ref, ks_ref, vs_ref):
    n = _rms(x_ref[...], g_mix_ref[...]).astype(BF16)
    pa = _dot(n, wa_ref[...])
    sb = _dot(n, wsb_ref[...])
    c_q = pa[:, :MLA_Q_RANK]
    c_kv = pa[:, MLA_Q_RANK:MLA_Q_RANK + MLA_KV_RANK]
    kpe = pa[:, MLA_Q_RANK + MLA_KV_RANK:]

    ang = pos_ref[...].astype(F32) * freq_ref[...]
    lane = lax.broadcasted_iota(jnp.int32, ang.shape, 1)
    cos_t = jnp.where(lane < MLA_NOPE + MLA_ROPE, jnp.cos(ang), 0.0)
    sin_t = jnp.sin(ang)

    def rope(tile):
        return tile * cos_t + pltpu.roll(tile, LANES - MLA_ROPE, 1) * sin_t

    nq = _rms(c_q, gq_ref[...]).astype(BF16)
    q = _dot(nq, wq_ref[...])
    nkv = _rms(c_kv, gkv_ref[...]).astype(BF16)
    kn = _dot(nkv, wk_ref[...])
    k_pe = rope(kpe)
    for h in range(MLA_HEADS):
        sl = slice(h * LANES, (h + 1) * LANES)
        qm_ref[:, sl] = (rope(q[:, sl]) * (MLA_SCALE * LOG2E)).astype(BF16)
        km_ref[:, sl] = (kn[:, sl] + k_pe).astype(BF16)
    vm_ref[...] = _dot(nkv, wv_ref[...]).astype(BF16).T

    w = SB_HEADS * SB_DIM
    qs_ref[...] = (sb[:, :w] * (SB_SCALE * LOG2E)).astype(BF16)
    ks_ref[...] = sb[:, w:2 * w].astype(BF16)
    vs_ref[...] = sb[:, 2 * w:].astype(BF16).T


def _const_spec(shape):
    return pl.BlockSpec(shape, lambda *_: (0,) * len(shape), pipeline_mode=pl.Buffered(1))


def _proj_call(x2, pos2, freq, g_mix, wa, wsb, gq, wq, gkv, wk, wv, *, tm):
    T, D = x2.shape
    row = lambda w: pl.BlockSpec((tm, w), lambda i: (i, 0))
    consts = (freq, g_mix, wa, wsb, gq, wq, gkv, wk, wv)
    col = lambda w: pl.BlockSpec((w, tm), lambda i: (0, i))
    w_mla, w_v, w_sb = MLA_HEADS * LANES, MLA_HEADS * MLA_V, SB_HEADS * SB_DIM
    tok = lambda w: jax.ShapeDtypeStruct((T, w), BF16)
    tok_t = lambda w: jax.ShapeDtypeStruct((w, T), BF16)
    return pl.pallas_call(
        _proj_kernel,
        grid=(T // tm,),
        in_specs=[row(D), row(1)] + [_const_spec(c.shape) for c in consts],
        out_specs=[row(w_mla), row(w_mla), col(w_v), row(w_sb), row(w_sb), col(w_sb)],
        out_shape=[tok(w_mla), tok(w_mla), tok_t(w_v), tok(w_sb), tok(w_sb), tok_t(w_sb)],
        compiler_params=pltpu.CompilerParams(dimension_semantics=("parallel",),
                                             vmem_limit_bytes=VMEM_LIMIT),
        name="proj",
    )(x2, pos2, *consts)


ONES_ROWS = 16
MAX_LAG = 64.0


def _mla_kernel(q_ref, k_ref, vt_ref, o_ref, *scratch, t, hp, wide, tiles):
    step_i = pl.program_id(2)

    def one_tile(tile, c):
        rows = pl.ds(pl.multiple_of(tile * t, t), t)
        _mla_tile(step_i * tiles + tile, q_ref.at[rows], k_ref, vt_ref, o_ref.at[rows], *scratch, t=t, hp=hp, wide=wide)
        return c

    lax.fori_loop(0, tiles, one_tile, 0)


def _mla_tile(qi, q_ref, k_ref, vt_ref, o_ref, acc_ref, m_ref, qt_ref, p_ref, scale_ref, lag_ref, *, t, hp, wide):
    for h in range(hp):
        qt_ref[h] = q_ref[:, h * LANES:(h + 1) * LANES].astype(F32).T.astype(BF16)
    key_l = lax.broadcasted_iota(jnp.int32, (t, t), 0)
    qry_l = lax.broadcasted_iota(jnp.int32, (t, t), 1)

    def scores(j, n_blk):
        k0 = pl.multiple_of(j * t, t)
        return [_dot(k_ref[pl.ds(k0, n_blk * t), h * LANES:(h + 1) * LANES], qt_ref[h]) for h in range(hp)]

    def values(h, j, n_blk):
        k0 = pl.multiple_of(j * t, t)
        w = n_blk * t
        return jnp.concatenate([vt_ref[h * MLA_V:(h + 1) * MLA_V, pl.ds(k0, w)],
                                jnp.ones((ONES_ROWS, w), BF16)], axis=0)

    def one_pass(h, s, diag_last, first):
        if diag_last:
            n_keys = s.shape[0]
            key_g = lax.broadcasted_iota(jnp.int32, (n_keys, t), 0)
            qry_g = lax.broadcasted_iota(jnp.int32, (n_keys, t), 1) + (n_keys - t)
            s = jnp.where(key_g <= qry_g, s, NEG_INF)
        m_old = s[0:1, :] if first else m_ref[h]
        p = jnp.exp2(s - m_old).astype(BF16)
        lag = jnp.maximum(jnp.max(s, axis=0, keepdims=True) - m_old, 0.0)
        m_ref[h] = m_old + lag
        lag_ref[h] = jnp.maximum(lag_ref[h], lag)
        return p, jnp.exp2(-lag)

    def add_pv(h, j, n_blk, p, scale):
        acc_ref[h] = (acc_ref[h] + _dot(values(h, j, n_blk), p)) * scale

    def refill(g, slot, s_all, first):
        for h in range(hp):
            p_ref[slot, h], scale_ref[slot, h] = one_pass(h, s_all[h], False, first)

    def back(g, slot):
        for h in range(hp):
            add_pv(h, g * wide, wide, p_ref[slot, h], scale_ref[slot, h])

    def pipelined(g, slot):
        s_all = scores(g * wide, wide)
        back(g - 1, 1 - slot)
        refill(g, slot, s_all, False)

    def body(i, c):
        pipelined(2 * i + 1, 1)
        pipelined(2 * i + 2, 0)
        return c

    def tail(n_blk, pending_slot):
        j = qi + 1 - n_blk
        s_all = scores(j, n_blk)
        if pending_slot is not None:
            back(n_groups - 1, pending_slot)
        for h in range(hp):
            add_pv(h, j, n_blk, *one_pass(h, s_all[h], True, pending_slot is None))

    acc_ref[...] = jnp.zeros_like(acc_ref)
    lag_ref[...] = jnp.zeros_like(lag_ref)
    n_groups = qi // wide
    n_pairs = jnp.maximum(n_groups - 1, 0) // 2

    @pl.when(n_groups >= 1)
    def _():
        refill(0, 0, scores(0, wide), True)

    lax.fori_loop(0, n_pairs, body, 0)

    @pl.when(jnp.logical_and(n_groups >= 2, n_groups % 2 == 0))
    def _():
        pipelined(n_groups - 1, 1)

    for n_blk in range(1, wide + 1):
        is_size = qi % wide == n_blk - 1
        pl.when(jnp.logical_and(is_size, n_groups == 0))(functools.partial(tail, n_blk, None))
        for slot in (0, 1):
            has_slot = jnp.logical_and(n_groups >= 1, (n_groups - 1) % 2 == slot)
            pl.when(jnp.logical_and(is_size, has_slot))(functools.partial(tail, n_blk, slot))

    def two_pass_step(j, masked):
        s_all = scores(j, 1)
        for h in range(hp):
            s = s_all[h]
            if masked:
                s = jnp.where(key_l <= qry_l, s, NEG_INF)
            m_prev = m_ref[h]
            m_new = jnp.maximum(m_prev, jnp.max(s, axis=0, keepdims=True))
            m_ref[h] = m_new
            acc_ref[h] = jnp.exp2(m_prev - m_new) * acc_ref[h] + _dot(values(h, j, 1), jnp.exp2(s - m_new).astype(BF16))

    @pl.when(jnp.max(lag_ref[...]) > MAX_LAG)
    def _():
        acc_ref[...] = jnp.zeros_like(acc_ref)
        m_ref[...] = jnp.full_like(m_ref, NEG_INF)
        lax.fori_loop(0, qi, lambda j, c: (two_pass_step(j, False), c)[1], 0)
        two_pass_step(qi, True)

    for pair in range(hp // HEAD_PAIR):
        outs = []
        for h in (pair * HEAD_PAIR, pair * HEAD_PAIR + 1):
            a = acc_ref[h]
            outs.append(a[:MLA_V] / a[MLA_V:MLA_V + 1])
        o_t = jnp.concatenate(outs, axis=0)
        o_ref[:, pair * LANES:(pair + 1) * LANES] = o_t.T.astype(o_ref.dtype)


def _mla_call(qm, km, vmt, *, t, hp, wide, tiles):
    B, S, _ = qm.shape
    return pl.pallas_call(
        functools.partial(_mla_kernel, t=t, hp=hp, wide=wide, tiles=tiles),
        grid=(B, MLA_HEADS // hp, S // (tiles * t)),
        in_specs=[pl.BlockSpec((None, tiles * t, hp * LANES), lambda b, g, i: (b, i, g)),
                  pl.BlockSpec((None, S, hp * LANES), lambda b, g, i: (b, 0, g)),
                  pl.BlockSpec((hp * MLA_V, S), lambda b, g, i: (g, b))],
        out_specs=pl.BlockSpec((None, tiles * t, hp * MLA_V), lambda b, g, i: (b, i, g)),
        out_shape=jax.ShapeDtypeStruct((B, S, MLA_HEADS * MLA_V), BF16),
        scratch_shapes=[pltpu.VMEM((hp, MLA_V + ONES_ROWS, t), F32),
                        pltpu.VMEM((hp, 1, t), F32),
                        pltpu.VMEM((hp, LANES, t), BF16),
                        pltpu.VMEM((2, hp, wide * t, t), BF16),
                        pltpu.VMEM((2, hp, 1, t), F32),
                        pltpu.VMEM((hp, 1, t), F32)],
        compiler_params=pltpu.CompilerParams(dimension_semantics=("parallel", "parallel", "arbitrary"),
                                             vmem_limit_bytes=VMEM_LIMIT),
        name="mla_attn",
    )(qm, km, vmt)


def _sb_kernel(q_ref, k_ref, vt_ref, tri_ref, o_ref, acc_ref, carry_ref, qt_ref, a_ref, tot_ref,
               *, t, hp, wide, tiles):
    step_i = pl.program_id(2)
    acc_ref[...] = jnp.zeros_like(acc_ref)
    carry_ref[...] = jnp.zeros_like(carry_ref)
    q_lane = lax.broadcasted_iota(jnp.int32, (t, LANES), 1)
    for tile in range(tiles):
        for pair in range(hp // HEAD_PAIR):
            q = q_ref[tile * t:(tile + 1) * t, pair * LANES:(pair + 1) * LANES].astype(F32)
            qt_ref[tile, pair * HEAD_PAIR] = jnp.where(q_lane < SB_DIM, q, 0.0).T.astype(BF16)
            qt_ref[tile, pair * HEAD_PAIR + 1] = jnp.where(q_lane >= SB_DIM, q, 0.0).T.astype(BF16)
    key_l = lax.broadcasted_iota(jnp.int32, (t, t), 0)
    qry_l = lax.broadcasted_iota(jnp.int32, (t, t), 1)

    valid = key_l < qry_l

    def tile_qi(tile):
        return step_i * tiles + tile

    def key_offset(j):
        return pl.multiple_of(j * t, t)

    def scores(items):
        zs = {}
        for n, (tile, j) in enumerate(items):
            for h in range(hp):
                pair = h // HEAD_PAIR
                k_blk = k_ref[pl.ds(key_offset(j), t), pair * LANES:(pair + 1) * LANES]
                zs[n, h] = _dot(k_blk, qt_ref[tile, h])
        return zs

    def suffix(z, masked):
        sp = jnp.maximum(z, jnp.log2(1.0 + jnp.exp2(jnp.minimum(z, SOFTPLUS_CLAMP))))
        if masked:
            sp = jnp.where(valid, sp, 0.0)
        c = _dot(tri_ref[...], sp.astype(BF16))
        a = jnp.exp2(z - c)
        if masked:
            a = jnp.where(valid, a, 0.0)
        return a.astype(BF16), c[0:1, :]

    def weights_pv(tile, h, j, a, total):
        vt = vt_ref[h * SB_DIM:(h + 1) * SB_DIM, pl.ds(key_offset(j), t)]
        carry = carry_ref[tile, h]
        acc_ref[tile, h] += _dot(vt, a) * jnp.exp2(carry)
        carry_ref[tile, h] = carry - total

    def step(items, masked):
        zs = scores(items)
        mid = {nh: suffix(z, masked) for nh, z in zs.items()}
        for n, (tile, j) in enumerate(items):
            for h in range(hp):
                weights_pv(tile, h, j, *mid[n, h])

    step([(tile, tile_qi(tile)) for tile in range(tiles)], True)

    @pl.when(step_i >= 1)
    def _():
        step([(tile, tile_qi(tile) - 1) for tile in range(tiles)], False)

    if tiles > 1:
        @pl.when(step_i == 0)
        def _():
            step([(tile, tile_qi(tile) - 1) for tile in range(1, tiles)], False)

    def rest_of_tile(tile, c):
        qi = tile_qi(tile)

        def group_items(g):
            return [(tile, qi - 2 - g * wide - b) for b in range(wide)]

        def refill(g, zs):
            for (b, h), z in zs.items():
                a_ref[g % 2, b, h], tot_ref[g % 2, b, h] = suffix(z, False)

        def back(g):
            for b, (_, j) in enumerate(group_items(g)):
                for h in range(hp):
                    weights_pv(tile, h, j, a_ref[g % 2, b, h], tot_ref[g % 2, b, h])

        def alive():
            return (jnp.max(carry_ref[tile]) > EXP2_IS_ZERO_BELOW).astype(jnp.int32)

        n_rest = jnp.maximum(qi - 1, 0)
        n_wide = n_rest // wide
        live_0 = alive()

        @pl.when(jnp.logical_and(n_wide >= 1, live_0 == 1))
        def _():
            refill(0, scores(group_items(0)))

        def cond(state):
            g, live = state
            return jnp.logical_and(g < n_wide, live == 1)

        def body(state):
            g, _ = state
            zs = scores(group_items(g))
            back(g - 1)
            refill(g, zs)
            return g + 1, alive()

        g_end, live = lax.while_loop(cond, body, (jnp.int32(1), live_0))

        @pl.when(jnp.logical_and(n_wide >= 1, live == 1))
        def _():
            back(g_end - 1)

        for r in range(wide - 1):
            @pl.when(jnp.logical_and(n_rest - n_wide * wide > r, live == 1))
            def _():
                step([(tile, qi - 2 - n_wide * wide - r)], False)
        return c

    lax.fori_loop(0, tiles, rest_of_tile, 0)

    for tile in range(tiles):
        for pair in range(hp // HEAD_PAIR):
            o_t = jnp.concatenate([acc_ref[tile, pair * HEAD_PAIR], acc_ref[tile, pair * HEAD_PAIR + 1]], axis=0)
            o_ref[tile * t:(tile + 1) * t, pair * LANES:(pair + 1) * LANES] = o_t.T.astype(o_ref.dtype)


def _sb_call(qs, ks, vst, *, t, hp, wide, tiles):
    B, S, _ = qs.shape
    r = lax.broadcasted_iota(jnp.int32, (t, t), 0)
    c = lax.broadcasted_iota(jnp.int32, (t, t), 1)
    tri = (c >= r).astype(BF16)
    w = hp * SB_DIM
    return pl.pallas_call(
        functools.partial(_sb_kernel, t=t, hp=hp, wide=wide, tiles=tiles),
        grid=(B, SB_HEADS // hp, S // (tiles * t)),
        in_specs=[pl.BlockSpec((None, tiles * t, w), lambda b, g, i: (b, i, g)),
                  pl.BlockSpec((None, S, w), lambda b, g, i: (b, 0, g)),
                  pl.BlockSpec((w, S), lambda b, g, i: (g, b)),
                  pl.BlockSpec((t, t), lambda b, g, i: (0, 0))],
        out_specs=pl.BlockSpec((None, tiles * t, w), lambda b, g, i: (b, i, g)),
        out_shape=jax.ShapeDtypeStruct((B, S, SB_HEADS * SB_DIM), BF16),
        scratch_shapes=[pltpu.VMEM((tiles, hp, SB_DIM, t), F32),
                        pltpu.VMEM((tiles, hp, 1, t), F32),
                        pltpu.VMEM((tiles, hp, LANES, t), BF16),
                        pltpu.VMEM((2, wide, hp, t, t), BF16),
                        pltpu.VMEM((2, wide, hp, 1, t), F32)],
        compiler_params=pltpu.CompilerParams(dimension_semantics=("parallel", "parallel", "arbitrary"),
                                             vmem_limit_bytes=VMEM_LIMIT),
        name="sb_attn",
    )(qs, ks, vst, tri)


def _post_kernel(x_ref, oa_ref, ob_ref, p_ref, g_mix_ref, wga_ref, wgb_ref, wbra_ref, wbrb_ref, wout_ref,
                 g_ffn_ref, wfg_ref, wfu_ref, wfd_ref, wpg_ref, wpp_ref, g_ple_ref, g_fin_ref, out_ref,
                 *, ff_chunk, final_norm):
    x = x_ref[...]
    n = _rms(x, g_mix_ref[...]).astype(BF16)
    merged = (jax.nn.sigmoid(_dot(n, wga_ref[...])) * _dot(oa_ref[...], wbra_ref[...])
              + jax.nn.sigmoid(_dot(n, wgb_ref[...])) * _dot(ob_ref[...], wbrb_ref[...]))
    h = x + _dot(merged.astype(BF16), wout_ref[...])

    n2 = _rms(h, g_ffn_ref[...]).astype(BF16)
    d_ff = wfg_ref.shape[1]
    ff = None
    for c0 in range(0, d_ff, ff_chunk):
        c1 = min(c0 + ff_chunk, d_ff)
        g = _dot(n2, wfg_ref[:, c0:c1])
        u = _dot(n2, wfu_ref[:, c0:c1])
        part = _dot((g * jax.nn.sigmoid(g) * u).astype(BF16), wfd_ref[c0:c1, :])
        ff = part if ff is None else ff + part
    h = h + ff

    e = _rms(_dot(p_ref[...].astype(BF16), wpp_ref[...]), g_ple_ref[...])
    h = h + jax.nn.sigmoid(_dot(h.astype(BF16), wpg_ref[...])) * e
    out_ref[...] = _rms(h, g_fin_ref[...]) if final_norm else h


def _post_call(x2, oa, ob, p2, consts, *, tm, ff_chunk, final_norm):
    T, D = x2.shape
    row = lambda w: pl.BlockSpec((tm, w), lambda i: (i, 0))
    return pl.pallas_call(
        functools.partial(_post_kernel, ff_chunk=ff_chunk, final_norm=final_norm),
        grid=(T // tm,),
        in_specs=[row(D), row(oa.shape[1]), row(ob.shape[1]), row(p2.shape[1])]
                 + [_const_spec(c.shape) for c in consts],
        out_specs=row(D),
        out_shape=jax.ShapeDtypeStruct((T, D), F32),
        compiler_params=pltpu.CompilerParams(dimension_semantics=("parallel",),
                                             vmem_limit_bytes=VMEM_LIMIT),
        name="post",
    )(x2, oa, ob, p2, *consts)


def _rotate_half_cols(w):
    half = w.shape[-1] // 2
    return jnp.concatenate([-w[..., half:], w[..., :half]], axis=-1)


def _layer_weights(w_in, w_q_b, w_kv_b):
    d = w_in.shape[0]
    o = 0
    cols = []
    for wd in (MLA_Q_RANK, MLA_KV_RANK, MLA_ROPE, SB_HEADS * SB_DIM, SB_HEADS * SB_DIM, SB_HEADS * SB_DIM, d, d):
        cols.append(w_in[:, o:o + wd])
        o += wd
    w_cq, w_ckv, w_kpe, w_qs, w_ks, w_vs, w_ga, w_gb = cols

    kpe_tile = jnp.concatenate([jnp.zeros_like(w_in[:, :MLA_NOPE]), w_kpe, _rotate_half_cols(w_kpe)], axis=1)
    wa = jnp.concatenate([w_cq, w_ckv, kpe_tile], axis=1)
    wsb = jnp.concatenate([w_qs, w_ks, w_vs], axis=1)

    qb = w_q_b.reshape(MLA_Q_RANK, MLA_HEADS, MLA_NOPE + MLA_ROPE)
    wq = jnp.concatenate([qb, _rotate_half_cols(qb[..., MLA_NOPE:])], axis=-1).reshape(MLA_Q_RANK, MLA_HEADS * LANES)

    kvb = w_kv_b.reshape(MLA_KV_RANK, MLA_HEADS, MLA_NOPE + MLA_V)
    wk = jnp.pad(kvb[..., :MLA_NOPE], ((0, 0), (0, 0), (0, LANES - MLA_NOPE))).reshape(MLA_KV_RANK, MLA_HEADS * LANES)
    wv = kvb[..., MLA_NOPE:].reshape(MLA_KV_RANK, MLA_HEADS * MLA_V)
    bf = lambda t: t.astype(BF16)
    return tuple(map(bf, (wa, wsb, wq, wk, wv, w_ga, w_gb)))


def _rope_freq_row():
    inv_freq = 1.0 / (ROPE_THETA ** (jnp.arange(0, MLA_ROPE, 2, dtype=F32) / MLA_ROPE))
    both = jnp.concatenate([inv_freq, inv_freq])
    return jnp.pad(both, (MLA_NOPE, LANES - MLA_NOPE - MLA_ROPE)).reshape(1, LANES)


def _tiles(B, S):
    T = B * S
    tm_proj = min(512, T)
    tm_post = min(512, T)
    t_attn = min(256, S)
    heads_per_step = 8
    wide = 2
    sb_tiles = 2
    mla_tiles = 4
    return tm_proj, tm_post, t_attn, heads_per_step, wide, sb_tiles, mla_tiles


def kernel(x, p, positions, g_mix, w_in, g_q_a, w_q_b, g_kv_a, w_kv_b, w_br_mla, w_br_sb, w_out, g_ffn,
           w_ffn_gate, w_ffn_up, w_ffn_down, w_ple_gate, w_ple_proj, g_ple, g_final):
    B, S, D = x.shape
    T = B * S
    depth = w_in.shape[0]
    tm_proj, tm_post, t_attn, hp, wide, sb_tiles, mla_tiles = _tiles(B, S)
    d_ff = w_ffn_gate.shape[-1]
    ff_chunk = min(d_ff, 1024)
    row = lambda g: g.reshape(1, -1).astype(F32)
    bf = lambda t: t.astype(BF16)

    pos2 = positions.reshape(T, 1).astype(jnp.int32)
    freq = _rope_freq_row()
    h = x.reshape(T, D)
    for i in range(depth):
        wa, wsb, wq, wk, wv, w_ga, w_gb = _layer_weights(w_in[i], w_q_b[i], w_kv_b[i])
        qm, km, vm, qs, ks, vs = _proj_call(h, pos2, freq, row(g_mix[i]), wa, wsb, row(g_q_a[i]), wq,
                                            row(g_kv_a[i]), wk, wv, tm=tm_proj)
        shp = lambda t: t.reshape(B, S, t.shape[-1])
        n_q = S // t_attn
        o_a = _mla_call(shp(qm), shp(km), vm, t=t_attn, hp=hp, wide=wide,
                        tiles=mla_tiles if n_q % mla_tiles == 0 else 1).reshape(T, -1)
        o_b = _sb_call(shp(qs), shp(ks), vs, t=t_attn, hp=hp, wide=wide,
                       tiles=sb_tiles if (S // t_attn) % sb_tiles == 0 else 1).reshape(T, -1)
        consts = (row(g_mix[i]), w_ga, w_gb, bf(w_br_mla[i]), bf(w_br_sb[i]), bf(w_out[i]), row(g_ffn[i]),
                  bf(w_ffn_gate[i]), bf(w_ffn_up[i]), bf(w_ffn_down[i]), bf(w_ple_gate[i]), bf(w_ple_proj[i]),
                  row(g_ple[i]), row(g_final))
        h = _post_call(h, o_a, o_b, p[i].reshape(T, -1), consts, tm=tm_post, ff_chunk=ff_chunk,
                       final_norm=(i == depth - 1))
    return h.reshape(B, S, D)
```

```python
import functools
import math

import jax
import jax.numpy as jnp
from jax import lax
from jax.experimental import pallas as pl
from jax.experimental.pallas import tpu as pltpu

EPS = 1e-6
MLA_HEADS = 8
MLA_NOPE = 64
MLA_ROPE = 32
MLA_V = 64
MLA_Q_RANK = 384
MLA_KV_RANK = 256
ROPE_THETA = 10000.0
MLA_SCALE = 1.0 / math.sqrt(MLA_NOPE + MLA_ROPE)
SB_HEADS = 8
SB_DIM = 64
SB_SCALE = 1.0 / math.sqrt(SB_DIM)
NEG_INF = -1e30
LOG2E = math.log2(math.e)
SOFTPLUS_CLAMP = 64.0
EXP2_IS_ZERO_BELOW = -151.0

LANES = 128
HEAD_PAIR = 2
VMEM_LIMIT = 56 * 1024 * 1024

F32 = jnp.float32
BF16 = jnp.bfloat16


def _rms(x, g):
    return x * lax.rsqrt(jnp.mean(x * x, axis=-1, keepdims=True) + EPS) * g


def _dot(a, b):
    return jnp.dot(a, b, preferred_element_type=F32)


def _dot_nt(a, b):
    return lax.dot_general(a, b, (((1,), (1,)), ((), ())), preferred_element_type=F32)


def _proj_kernel(x_ref, pos_ref, freq_ref, g_mix_ref, wa_ref, wsb_ref, gq_ref, wq_ref,
                 gkv_ref, wk_ref, wv_ref,
                 qm_ref, km_ref, vm_ref, qs_ref, ks_ref, vs_ref):
    n = _rms(x_ref[...], g_mix_ref[...]).astype(BF16)
    pa = _dot(n, wa_ref[...])
    sb = _dot(n, wsb_ref[...])
    c_q = pa[:, :MLA_Q_RANK]
    c_kv = pa[:, MLA_Q_RANK:MLA_Q_RANK + MLA_KV_RANK]
    kpe = pa[:, MLA_Q_RANK + MLA_KV_RANK:]

    ang = pos_ref[...].astype(F32) * freq_ref[...]
    lane = lax.broadcasted_iota(jnp.int32, ang.shape, 1)
    cos_t = jnp.where(lane < MLA_NOPE + MLA_ROPE, jnp.cos(ang), 0.0)
    sin_t = jnp.sin(ang)

    def rope(tile):
        return tile * cos_t + pltpu.roll(tile, LANES - MLA_ROPE, 1) * sin_t

    nq = _rms(c_q, gq_ref[...]).astype(BF16)
    q = _dot(nq, wq_ref[...])
    nkv = _rms(c_kv, gkv_ref[...]).astype(BF16)
    kn = _dot(nkv, wk_ref[...])
    k_pe = rope(kpe)
    for h in range(MLA_HEADS):
        sl = slice(h * LANES, (h + 1) * LANES)
        qm_ref[:, sl] = (rope(q[:, sl]) * (MLA_SCALE * LOG2E)).astype(BF16)
        km_ref[:, sl] = (kn[:, sl] + k_pe).astype(BF16)
    vm_ref[...] = _dot(nkv, wv_ref[...]).astype(BF16).T

    w = SB_HEADS * SB_DIM
    qs_ref[...] = (sb[:, :w] * (SB_SCALE * LOG2E)).astype(BF16)
    ks_ref[...] = sb[:, w:2 * w].astype(BF16)
    vs_ref[...] = sb[:, 2 * w:].astype(BF16).T


def _const_spec(shape):
    return pl.BlockSpec(shape, lambda *_: (0,) * len(shape), pipeline_mode=pl.Buffered(1))


def _proj_call(x2, pos2, freq, g_mix, wa, wsb, gq, wq, gkv, wk, wv, *, tm):
    T, D = x2.shape
    row = lambda w: pl.BlockSpec((tm, w), lambda i: (i, 0))
    consts = (freq, g_mix, wa, wsb, gq, wq, gkv, wk, wv)
    col = lambda w: pl.BlockSpec((w, tm), lambda i: (0, i))
    w_mla, w_v, w_sb = MLA_HEADS * LANES, MLA_HEADS * MLA_V, SB_HEADS * SB_DIM
    tok = lambda w: jax.ShapeDtypeStruct((T, w), BF16)
    tok_t = lambda w: jax.ShapeDtypeStruct((w, T), BF16)
    return pl.pallas_call(
        _proj_kernel,
        grid=(T // tm,),
        in_specs=[row(D), row(1)] + [_const_spec(c.shape) for c in consts],
        out_specs=[row(w_mla), row(w_mla), col(w_v), row(w_sb), row(w_sb), col(w_sb)],
        out_shape=[tok(w_mla), tok(w_mla), tok_t(w_v), tok(w_sb), tok(w_sb), tok_t(w_sb)],
        compiler_params=pltpu.CompilerParams(dimension_semantics=("parallel",),
                                             vmem_limit_bytes=VMEM_LIMIT),
        name="proj",
    )(x2, pos2, *consts)


ONES_ROWS = 16
MAX_LAG = 64.0


def _mla_kernel(q_ref, k_ref, vt_ref, o_ref, acc_ref, m_ref, qt_ref, p_ref, scale_ref, lag_ref, *, t, hp, wide):
    qi = pl.program_id(2)
    for h in range(hp):
        qt_ref[h] = q_ref[:, h * LANES:(h + 1) * LANES].astype(F32).T.astype(BF16)
    key_l = lax.broadcasted_iota(jnp.int32, (t, t), 0)
    qry_l = lax.broadcasted_iota(jnp.int32, (t, t), 1)

    def scores(j, n_blk):
        k0 = pl.multiple_of(j * t, t)
        return [_dot(k_ref[pl.ds(k0, n_blk * t), h * LANES:(h + 1) * LANES], qt_ref[h]) for h in range(hp)]

    def values(h, j, n_blk):
        k0 = pl.multiple_of(j * t, t)
        w = n_blk * t
        return jnp.concatenate([vt_ref[h * MLA_V:(h + 1) * MLA_V, pl.ds(k0, w)],
                                jnp.ones((ONES_ROWS, w), BF16)], axis=0)

    def one_pass(h, s, diag_last, first):
        if diag_last:
            n_keys = s.shape[0]
            key_g = lax.broadcasted_iota(jnp.int32, (n_keys, t), 0)
            qry_g = lax.broadcasted_iota(jnp.int32, (n_keys, t), 1) + (n_keys - t)
            s = jnp.where(key_g <= qry_g, s, NEG_INF)
        m_old = s[0:1, :] if first else m_ref[h]
        p = jnp.exp2(s - m_old).astype(BF16)
        lag = jnp.maximum(jnp.max(s, axis=0, keepdims=True) - m_old, 0.0)
        m_ref[h] = m_old + lag
        lag_ref[h] = jnp.maximum(lag_ref[h], lag)
        return p, jnp.exp2(-lag)

    def add_pv(h, j, n_blk, p, scale):
        acc_ref[h] = (acc_ref[h] + _dot(values(h, j, n_blk), p)) * scale

    def refill(g, slot, s_all, first):
        for h in range(hp):
            p_ref[slot, h], scale_ref[slot, h] = one_pass(h, s_all[h], False, first)

    def back(g, slot):
        for h in range(hp):
            add_pv(h, g * wide, wide, p_ref[slot, h], scale_ref[slot, h])

    def pipelined(g, slot):
        s_all = scores(g * wide, wide)
        back(g - 1, 1 - slot)
        refill(g, slot, s_all, False)

    def body(i, c):
        pipelined(2 * i + 1, 1)
        pipelined(2 * i + 2, 0)
        return c

    def tail(n_blk, pending_slot):
        j = qi + 1 - n_blk
        s_all = scores(j, n_blk)
        if pending_slot is not None:
            back(n_groups - 1, pending_slot)
        for h in range(hp):
            add_pv(h, j, n_blk, *one_pass(h, s_all[h], True, pending_slot is None))

    acc_ref[...] = jnp.zeros_like(acc_ref)
    lag_ref[...] = jnp.zeros_like(lag_ref)
    n_groups = qi // wide
    n_pairs = jnp.maximum(n_groups - 1, 0) // 2

    @pl.when(n_groups >= 1)
    def _():
        refill(0, 0, scores(0, wide), True)

    lax.fori_loop(0, n_pairs, body, 0)

    @pl.when(jnp.logical_and(n_groups >= 2, n_groups % 2 == 0))
    def _():
        pipelined(n_groups - 1, 1)

    for n_blk in range(1, wide + 1):
        is_size = qi % wide == n_blk - 1
        pl.when(jnp.logical_and(is_size, n_groups == 0))(functools.partial(tail, n_blk, None))
        for slot in (0, 1):
            has_slot = jnp.logical_and(n_groups >= 1, (n_groups - 1) % 2 == slot)
            pl.when(jnp.logical_and(is_size, has_slot))(functools.partial(tail, n_blk, slot))

    def two_pass_step(j, masked):
        s_all = scores(j, 1)
        for h in range(hp):
            s = s_all[h]
            if masked:
                s = jnp.where(key_l <= qry_l, s, NEG_INF)
            m_prev = m_ref[h]
            m_new = jnp.maximum(m_prev, jnp.max(s, axis=0, keepdims=True))
            m_ref[h] = m_new
            acc_ref[h] = jnp.exp2(m_prev - m_new) * acc_ref[h] + _dot(values(h, j, 1), jnp.exp2(s - m_new).astype(BF16))

    @pl.when(jnp.max(lag_ref[...]) > MAX_LAG)
    def _():
        acc_ref[...] = jnp.zeros_like(acc_ref)
        m_ref[...] = jnp.full_like(m_ref, NEG_INF)
        lax.fori_loop(0, qi, lambda j, c: (two_pass_step(j, False), c)[1], 0)
        two_pass_step(qi, True)

    for pair in range(hp // HEAD_PAIR):
        outs = []
        for h in (pair * HEAD_PAIR, pair * HEAD_PAIR + 1):
            a = acc_ref[h]
            outs.append(a[:MLA_V] / a[MLA_V:MLA_V + 1])
        o_t = jnp.concatenate(outs, axis=0)
        o_ref[:, pair * LANES:(pair + 1) * LANES] = o_t.T.astype(o_ref.dtype)


def _mla_call(qm, km, vmt, *, t, hp, wide):
    B, S, _ = qm.shape
    return pl.pallas_call(
        functools.partial(_mla_kernel, t=t, hp=hp, wide=wide),
        grid=(B, MLA_HEADS // hp, S // t),
        in_specs=[pl.BlockSpec((None, t, hp * LANES), lambda b, g, qi: (b, qi, g)),
                  pl.BlockSpec((None, S, hp * LANES), lambda b, g, qi: (b, 0, g)),
                  pl.BlockSpec((hp * MLA_V, S), lambda b, g, qi: (g, b))],
        out_specs=pl.BlockSpec((None, t, hp * MLA_V), lambda b, g, qi: (b, qi, g)),
        out_shape=jax.ShapeDtypeStruct((B, S, MLA_HEADS * MLA_V), BF16),
        scratch_shapes=[pltpu.VMEM((hp, MLA_V + ONES_ROWS, t), F32),
                        pltpu.VMEM((hp, 1, t), F32),
                        pltpu.VMEM((hp, LANES, t), BF16),
                        pltpu.VMEM((2, hp, wide * t, t), BF16),
                        pltpu.VMEM((2, hp, 1, t), F32),
                        pltpu.VMEM((hp, 1, t), F32)],
        compiler_params=pltpu.CompilerParams(dimension_semantics=("parallel", "parallel", "arbitrary"),
                                             vmem_limit_bytes=VMEM_LIMIT),
        name="mla_attn",
    )(qm, km, vmt)


def _sb_kernel(q_ref, k_ref, vt_ref, tri_ref, o_ref, acc_ref, carry_ref, qt_ref, a_ref, tot_ref,
               *, t, hp, wide, tiles):
    step_i = pl.program_id(2)
    acc_ref[...] = jnp.zeros_like(acc_ref)
    carry_ref[...] = jnp.zeros_like(carry_ref)
    q_lane = lax.broadcasted_iota(jnp.int32, (t, LANES), 1)
    for tile in range(tiles):
        for pair in range(hp // HEAD_PAIR):
            q = q_ref[tile * t:(tile + 1) * t, pair * LANES:(pair + 1) * LANES].astype(F32)
            qt_ref[tile, pair * HEAD_PAIR] = jnp.where(q_lane < SB_DIM, q, 0.0).T.astype(BF16)
            qt_ref[tile, pair * HEAD_PAIR + 1] = jnp.where(q_lane >= SB_DIM, q, 0.0).T.astype(BF16)
    key_l = lax.broadcasted_iota(jnp.int32, (t, t), 0)
    qry_l = lax.broadcasted_iota(jnp.int32, (t, t), 1)

    valid = key_l < qry_l

    def tile_qi(tile):
        return step_i * tiles + tile

    def key_offset(j):
        return pl.multiple_of(j * t, t)

    def scores(items):
        zs = {}
        for n, (tile, j) in enumerate(items):
            for h in range(hp):
                pair = h // HEAD_PAIR
                k_blk = k_ref[pl.ds(key_offset(j), t), pair * LANES:(pair + 1) * LANES]
                zs[n, h] = _dot(k_blk, qt_ref[tile, h])
        return zs

    def suffix(z, masked):
        sp = jnp.maximum(z, jnp.log2(1.0 + jnp.exp2(jnp.minimum(z, SOFTPLUS_CLAMP))))
        if masked:
            sp = jnp.where(valid, sp, 0.0)
        c = _dot(tri_ref[...], sp.astype(BF16))
        a = jnp.exp2(z - c)
        if masked:
            a = jnp.where(valid, a, 0.0)
        return a.astype(BF16), c[0:1, :]

    def weights_pv(tile, h, j, a, total):
        vt = vt_ref[h * SB_DIM:(h + 1) * SB_DIM, pl.ds(key_offset(j), t)]
        carry = carry_ref[tile, h]
        acc_ref[tile, h] += _dot(vt, a) * jnp.exp2(carry)
        carry_ref[tile, h] = carry - total

    def step(items, masked):
        zs = scores(items)
        mid = {nh: suffix(z, masked) for nh, z in zs.items()}
        for n, (tile, j) in enumerate(items):
            for h in range(hp):
                weights_pv(tile, h, j, *mid[n, h])

    step([(tile, tile_qi(tile)) for tile in range(tiles)], True)

    @pl.when(step_i >= 1)
    def _():
        step([(tile, tile_qi(tile) - 1) for tile in range(tiles)], False)

    if tiles > 1:
        @pl.when(step_i == 0)
        def _():
            step([(tile, tile_qi(tile) - 1) for tile in range(1, tiles)], False)

    def rest_of_tile(tile, c):
        qi = tile_qi(tile)

        def group_items(g):
            return [(tile, qi - 2 - g * wide - b) for b in range(wide)]

        def refill(g, zs):
            for (b, h), z in zs.items():
                a_ref[g % 2, b, h], tot_ref[g % 2, b, h] = suffix(z, False)

        def back(g):
            for b, (_, j) in enumerate(group_items(g)):
                for h in range(hp):
                    weights_pv(tile, h, j, a_ref[g % 2, b, h], tot_ref[g % 2, b, h])

        def alive():
            return (jnp.max(carry_ref[tile]) > EXP2_IS_ZERO_BELOW).astype(jnp.int32)

        n_rest = jnp.maximum(qi - 1, 0)
        n_wide = n_rest // wide
        live_0 = alive()

        @pl.when(jnp.logical_and(n_wide >= 1, live_0 == 1))
        def _():
            refill(0, scores(group_items(0)))

        def cond(state):
            g, live = state
            return jnp.logical_and(g < n_wide, live == 1)

        def body(state):
            g, _ = state
            zs = scores(group_items(g))
            back(g - 1)
            refill(g, zs)
            return g + 1, alive()

        g_end, live = lax.while_loop(cond, body, (jnp.int32(1), live_0))

        @pl.when(jnp.logical_and(n_wide >= 1, live == 1))
        def _():
            back(g_end - 1)

        for r in range(wide - 1):
            @pl.when(jnp.logical_and(n_rest - n_wide * wide > r, live == 1))
            def _():
                step([(tile, qi - 2 - n_wide * wide - r)], False)
        return c

    lax.fori_loop(0, tiles, rest_of_tile, 0)

    for tile in range(tiles):
        for pair in range(hp // HEAD_PAIR):
            o_t = jnp.concatenate([acc_ref[tile, pair * HEAD_PAIR], acc_ref[tile, pair * HEAD_PAIR + 1]], axis=0)
            o_ref[tile * t:(tile + 1) * t, pair * LANES:(pair + 1) * LANES] = o_t.T.astype(o_ref.dtype)


def _sb_call(qs, ks, vst, *, t, hp, wide, tiles):
    B, S, _ = qs.shape
    r = lax.broadcasted_iota(jnp.int32, (t, t), 0)
    c = lax.broadcasted_iota(jnp.int32, (t, t), 1)
    tri = (c >= r).astype(BF16)
    w = hp * SB_DIM
    return pl.pallas_call(
        functools.partial(_sb_kernel, t=t, hp=hp, wide=wide, tiles=tiles),
        grid=(B, SB_HEADS // hp, S // (tiles * t)),
        in_specs=[pl.BlockSpec((None, tiles * t, w), lambda b, g, i: (b, i, g)),
                  pl.BlockSpec((None, S, w), lambda b, g, i: (b, 0, g)),
                  pl.BlockSpec((w, S), lambda b, g, i: (g, b)),
                  pl.BlockSpec((t, t), lambda b, g, i: (0, 0))],
        out_specs=pl.BlockSpec((None, tiles * t, w), lambda b, g, i: (b, i, g)),
        out_shape=jax.ShapeDtypeStruct((B, S, SB_HEADS * SB_DIM), BF16),
        scratch_shapes=[pltpu.VMEM((tiles, hp, SB_DIM, t), F32),
                        pltpu.VMEM((tiles, hp, 1, t), F32),
                        pltpu.VMEM((tiles, hp, LANES, t), BF16),
                        pltpu.VMEM((2, wide, hp, t, t), BF16),
                        pltpu.VMEM((2, wide, hp, 1, t), F32)],
        compiler_params=pltpu.CompilerParams(dimension_semantics=("parallel", "parallel", "arbitrary"),
                                             vmem_limit_bytes=VMEM_LIMIT),
        name="sb_attn",
    )(qs, ks, vst, tri)


def _post_kernel(x_ref, oa_ref, ob_ref, p_ref, g_mix_ref, wga_ref, wgb_ref, wbra_ref, wbrb_ref, wout_ref,
                 g_ffn_ref, wfg_ref, wfu_ref, wfd_ref, wpg_ref, wpp_ref, g_ple_ref, g_fin_ref, out_ref,
                 *, ff_chunk, final_norm):
    x = x_ref[...]
    n = _rms(x, g_mix_ref[...]).astype(BF16)
    merged = (jax.nn.sigmoid(_dot(n, wga_ref[...])) * _dot(oa_ref[...], wbra_ref[...])
              + jax.nn.sigmoid(_dot(n, wgb_ref[...])) * _dot(ob_ref[...], wbrb_ref[...]))
    h = x + _dot(merged.astype(BF16), wout_ref[...])

    n2 = _rms(h, g_ffn_ref[...]).astype(BF16)
    d_ff = wfg_ref.shape[1]
    ff = None
    for c0 in range(0, d_ff, ff_chunk):
        c1 = min(c0 + ff_chunk, d_ff)
        g = _dot(n2, wfg_ref[:, c0:c1])
        u = _dot(n2, wfu_ref[:, c0:c1])
        part = _dot((g * jax.nn.sigmoid(g) * u).astype(BF16), wfd_ref[c0:c1, :])
        ff = part if ff is None else ff + part
    h = h + ff

    e = _rms(_dot(p_ref[...].astype(BF16), wpp_ref[...]), g_ple_ref[...])
    h = h + jax.nn.sigmoid(_dot(h.astype(BF16), wpg_ref[...])) * e
    out_ref[...] = _rms(h, g_fin_ref[...]) if final_norm else h


def _post_call(x2, oa, ob, p2, consts, *, tm, ff_chunk, final_norm):
    T, D = x2.shape
    row = lambda w: pl.BlockSpec((tm, w), lambda i: (i, 0))
    return pl.pallas_call(
        functools.partial(_post_kernel, ff_chunk=ff_chunk, final_norm=final_norm),
        grid=(T // tm,),
        in_specs=[row(D), row(oa.shape[1]), row(ob.shape[1]), row(p2.shape[1])]
                 + [_const_spec(c.shape) for c in consts],
        out_specs=row(D),
        out_shape=jax.ShapeDtypeStruct((T, D), F32),
        compiler_params=pltpu.CompilerParams(dimension_semantics=("parallel",),
                                             vmem_limit_bytes=VMEM_LIMIT),
        name="post",
    )(x2, oa, ob, p2, *consts)


def _rotate_half_cols(w):
    half = w.shape[-1] // 2
    return jnp.concatenate([-w[..., half:], w[..., :half]], axis=-1)


def _layer_weights(w_in, w_q_b, w_kv_b):
    d = w_in.shape[0]
    o = 0
    cols = []
    for wd in (MLA_Q_RANK, MLA_KV_RANK, MLA_ROPE, SB_HEADS * SB_DIM, SB_HEADS * SB_DIM, SB_HEADS * SB_DIM, d, d):
        cols.append(w_in[:, o:o + wd])
        o += wd
    w_cq, w_ckv, w_kpe, w_qs, w_ks, w_vs, w_ga, w_gb = cols

    kpe_tile = jnp.concatenate([jnp.zeros_like(w_in[:, :MLA_NOPE]), w_kpe, _rotate_half_cols(w_kpe)], axis=1)
    wa = jnp.concatenate([w_cq, w_ckv, kpe_tile], axis=1)
    wsb = jnp.concatenate([w_qs, w_ks, w_vs], axis=1)

    qb = w_q_b.reshape(MLA_Q_RANK, MLA_HEADS, MLA_NOPE + MLA_ROPE)
    wq = jnp.concatenate([qb, _rotate_half_cols(qb[..., MLA_NOPE:])], axis=-1).reshape(MLA_Q_RANK, MLA_HEADS * LANES)

    kvb = w_kv_b.reshape(MLA_KV_RANK, MLA_HEADS, MLA_NOPE + MLA_V)
    wk = jnp.pad(kvb[..., :MLA_NOPE], ((0, 0), (0, 0), (0, LANES - MLA_NOPE))).reshape(MLA_KV_RANK, MLA_HEADS * LANES)
    wv = kvb[..., MLA_NOPE:].reshape(MLA_KV_RANK, MLA_HEADS * MLA_V)
    bf = lambda t: t.astype(BF16)
    return tuple(map(bf, (wa, wsb, wq, wk, wv, w_ga, w_gb)))


def _rope_freq_row():
    inv_freq = 1.0 / (ROPE_THETA ** (jnp.arange(0, MLA_ROPE, 2, dtype=F32) / MLA_ROPE))
    both = jnp.concatenate([inv_freq, inv_freq])
    return jnp.pad(both, (MLA_NOPE, LANES - MLA_NOPE - MLA_ROPE)).reshape(1, LANES)


def _tiles(B, S):
    T = B * S
    tm_proj = min(512, T)
    tm_post = min(512, T)
    t_sb = min(256, S)
    t_mla = min(512, S)
    heads_per_step = 8
    sb_wide = 2
    mla_wide = 1
    sb_tiles = 2
    return tm_proj, tm_post, t_sb, t_mla, heads_per_step, sb_wide, mla_wide, sb_tiles


def kernel(x, p, positions, g_mix, w_in, g_q_a, w_q_b, g_kv_a, w_kv_b, w_br_mla, w_br_sb, w_out, g_ffn,
           w_ffn_gate, w_ffn_up, w_ffn_down, w_ple_gate, w_ple_proj, g_ple, g_final):
    B, S, D = x.shape
    T = B * S
    depth = w_in.shape[0]
    tm_proj, tm_post, t_sb, t_mla, hp, sb_wide, mla_wide, sb_tiles = _tiles(B, S)
    d_ff = w_ffn_gate.shape[-1]
    ff_chunk = min(d_ff, 1024)
    row = lambda g: g.reshape(1, -1).astype(F32)
    bf = lambda t: t.astype(BF16)

    pos2 = positions.reshape(T, 1).astype(jnp.int32)
    freq = _rope_freq_row()
    h = x.reshape(T, D)
    for i in range(depth):
        wa, wsb, wq, wk, wv, w_ga, w_gb = _layer_weights(w_in[i], w_q_b[i], w_kv_b[i])
        qm, km, vm, qs, ks, vs = _proj_call(h, pos2, freq, row(g_mix[i]), wa, wsb, row(g_q_a[i]), wq,
                                            row(g_kv_a[i]), wk, wv, tm=tm_proj)
        shp = lambda t: t.reshape(B, S, t.shape[-1])
        o_a = _mla_call(shp(qm), shp(km), vm, t=t_mla, hp=hp, wide=mla_wide).reshape(T, -1)
        o_b = _sb_call(shp(qs), shp(ks), vs, t=t_sb, hp=hp, wide=sb_wide,
                       tiles=sb_tiles if (S // t_sb) % sb_tiles == 0 else 1).reshape(T, -1)
        consts = (row(g_mix[i]), w_ga, w_gb, bf(w_br_mla[i]), bf(w_br_sb[i]), bf(w_out[i]), row(g_ffn[i]),
                  bf(w_ffn_gate[i]), bf(w_ffn_up[i]), bf(w_ffn_down[i]), bf(w_ple_gate[i]), bf(w_ple_proj[i]),
                  row(g_ple[i]), row(g_final))
        h = _post_call(h, o_a, o_b, p[i].reshape(T, -1), consts, tm=tm_post, ff_chunk=ff_chunk,
                       final_norm=(i == depth - 1))
    return h.reshape(B, S, D)
```

```python
import functools
import math

import jax
import jax.numpy as jnp
from jax import lax
from jax.experimental import pallas as pl
from jax.experimental.pallas import tpu as pltpu

EPS = 1e-6
MLA_HEADS = 8
MLA_NOPE = 64
MLA_ROPE = 32
MLA_V = 64
MLA_Q_RANK = 384
MLA_KV_RANK = 256
ROPE_THETA = 10000.0
MLA_SCALE = 1.0 / math.sqrt(MLA_NOPE + MLA_ROPE)
SB_HEADS = 8
SB_DIM = 64
SB_SCALE = 1.0 / math.sqrt(SB_DIM)
NEG_INF = -1e30
LOG2E = math.log2(math.e)
SOFTPLUS_CLAMP = 64.0
EXP2_IS_ZERO_BELOW = -151.0

LANES = 128
HEAD_PAIR = 2
VMEM_LIMIT = 56 * 1024 * 1024

F32 = jnp.float32
BF16 = jnp.bfloat16


def _rms(x, g):
    return x * lax.rsqrt(jnp.mean(x * x, axis=-1, keepdims=True) + EPS) * g


def _dot(a, b):
    return jnp.dot(a, b, preferred_element_type=F32)


def _dot_nt(a, b):
    return lax.dot_general(a, b, (((1,), (1,)), ((), ())), preferred_element_type=F32)


def _proj_kernel(x_ref, pos_ref, freq_ref, g_mix_ref, wa_ref, wsb_ref, gq_ref, wq_ref,
                 gkv_ref, wk_ref, wv_ref,
                 qm_ref, km_ref, vm_ref, qs_ref, ks_ref, vs_ref):
    n = _rms(x_ref[...], g_mix_ref[...]).astype(BF16)
    pa = _dot(n, wa_ref[...])
    sb = _dot(n, wsb_ref[...])
    c_q = pa[:, :MLA_Q_RANK]
    c_kv = pa[:, MLA_Q_RANK:MLA_Q_RANK + MLA_KV_RANK]
    kpe = pa[:, MLA_Q_RANK + MLA_KV_RANK:]

    ang = pos_ref[...].astype(F32) * freq_ref[...]
    lane = lax.broadcasted_iota(jnp.int32, ang.shape, 1)
    cos_t = jnp.where(lane < MLA_NOPE + MLA_ROPE, jnp.cos(ang), 0.0)
    sin_t = jnp.sin(ang)

    def rope(tile):
        return tile * cos_t + pltpu.roll(tile, LANES - MLA_ROPE, 1) * sin_t

    nq = _rms(c_q, gq_ref[...]).astype(BF16)
    q = _dot(nq, wq_ref[...])
    nkv = _rms(c_kv, gkv_ref[...]).astype(BF16)
    kn = _dot(nkv, wk_ref[...])
    k_pe = rope(kpe)
    for h in range(MLA_HEADS):
        sl = slice(h * LANES, (h + 1) * LANES)
        qm_ref[:, sl] = (rope(q[:, sl]) * (MLA_SCALE * LOG2E)).astype(BF16)
        km_ref[:, sl] = (kn[:, sl] + k_pe).astype(BF16)
    vm_ref[...] = _dot(nkv, wv_ref[...]).astype(BF16).T

    w = SB_HEADS * SB_DIM
    qs_ref[...] = (sb[:, :w] * (SB_SCALE * LOG2E)).astype(BF16)
    ks_ref[...] = sb[:, w:2 * w].astype(BF16)
    vs_ref[...] = sb[:, 2 * w:].astype(BF16).T


def _const_spec(shape):
    return pl.BlockSpec(shape, lambda *_: (0,) * len(shape), pipeline_mode=pl.Buffered(1))


def _proj_call(x2, pos2, freq, g_mix, wa, wsb, gq, wq, gkv, wk, wv, *, tm):
    T, D = x2.shape
    row = lambda w: pl.BlockSpec((tm, w), lambda i: (i, 0))
    consts = (freq, g_mix, wa, wsb, gq, wq, gkv, wk, wv)
    col = lambda w: pl.BlockSpec((w, tm), lambda i: (0, i))
    w_mla, w_v, w_sb = MLA_HEADS * LANES, MLA_HEADS * MLA_V, SB_HEADS * SB_DIM
    tok = lambda w: jax.ShapeDtypeStruct((T, w), BF16)
    tok_t = lambda w: jax.ShapeDtypeStruct((w, T), BF16)
    return pl.pallas_call(
        _proj_kernel,
        grid=(T // tm,),
        in_specs=[row(D), row(1)] + [_const_spec(c.shape) for c in consts],
        out_specs=[row(w_mla), row(w_mla), col(w_v), row(w_sb), row(w_sb), col(w_sb)],
        out_shape=[tok(w_mla), tok(w_mla), tok_t(w_v), tok(w_sb), tok(w_sb), tok_t(w_sb)],
        compiler_params=pltpu.CompilerParams(dimension_semantics=("parallel",),
                                             vmem_limit_bytes=VMEM_LIMIT),
        name="proj",
    )(x2, pos2, *consts)


ONES_ROWS = 16
MAX_LAG = 64.0


def _mla_kernel(q_ref, k_ref, vt_ref, o_ref, acc_ref, m_ref, qt_ref, p_ref, scale_ref, lag_ref, *, t, hp, wide):
    qi = pl.program_id(2)
    for h in range(hp):
        qt_ref[h] = q_ref[:, h * LANES:(h + 1) * LANES].astype(F32).T.astype(BF16)
    key_l = lax.broadcasted_iota(jnp.int32, (t, t), 0)
    qry_l = lax.broadcasted_iota(jnp.int32, (t, t), 1)

    def scores(j, n_blk):
        k0 = pl.multiple_of(j * t, t)
        return [_dot(k_ref[pl.ds(k0, n_blk * t), h * LANES:(h + 1) * LANES], qt_ref[h]) for h in range(hp)]

    def values(h, j, n_blk):
        k0 = pl.multiple_of(j * t, t)
        w = n_blk * t
        return jnp.concatenate([vt_ref[h * MLA_V:(h + 1) * MLA_V, pl.ds(k0, w)],
                                jnp.ones((ONES_ROWS, w), BF16)], axis=0)

    def one_pass(h, s, diag_last, first):
        if diag_last:
            n_keys = s.shape[0]
            key_g = lax.broadcasted_iota(jnp.int32, (n_keys, t), 0)
            qry_g = lax.broadcasted_iota(jnp.int32, (n_keys, t), 1) + (n_keys - t)
            s = jnp.where(key_g <= qry_g, s, NEG_INF)
        m_old = s[0:1, :] if first else m_ref[h]
        p = jnp.exp2(s - m_old).astype(BF16)
        lag = jnp.maximum(jnp.max(s, axis=0, keepdims=True) - m_old, 0.0)
        m_ref[h] = m_old + lag
        lag_ref[h] = jnp.maximum(lag_ref[h], lag)
        return p, jnp.exp2(-lag)

    def add_pv(h, j, n_blk, p, scale):
        acc_ref[h] = (acc_ref[h] + _dot(values(h, j, n_blk), p)) * scale

    def refill(g, slot, s_all, first):
        for h in range(hp):
            p_ref[slot, h], scale_ref[slot, h] = one_pass(h, s_all[h], False, first)

    def back(g, slot):
        for h in range(hp):
            add_pv(h, g * wide, wide, p_ref[slot, h], scale_ref[slot, h])

    def pipelined(g, slot):
        s_all = scores(g * wide, wide)
        back(g - 1, 1 - slot)
        refill(g, slot, s_all, False)

    def body(i, c):
        pipelined(2 * i + 1, 1)
        pipelined(2 * i + 2, 0)
        return c

    def tail(n_blk, pending_slot):
        j = qi + 1 - n_blk
        s_all = scores(j, n_blk)
        if pending_slot is not None:
            back(n_groups - 1, pending_slot)
        for h in range(hp):
            add_pv(h, j, n_blk, *one_pass(h, s_all[h], True, pending_slot is None))

    acc_ref[...] = jnp.zeros_like(acc_ref)
    lag_ref[...] = jnp.zeros_like(lag_ref)
    n_groups = qi // wide
    n_pairs = jnp.maximum(n_groups - 1, 0) // 2

    @pl.when(n_groups >= 1)
    def _():
        refill(0, 0, scores(0, wide), True)

    lax.fori_loop(0, n_pairs, body, 0)

    @pl.when(jnp.logical_and(n_groups >= 2, n_groups % 2 == 0))
    def _():
        pipelined(n_groups - 1, 1)

    for n_blk in range(1, wide + 1):
        is_size = qi % wide == n_blk - 1
        pl.when(jnp.logical_and(is_size, n_groups == 0))(functools.partial(tail, n_blk, None))
        for slot in (0, 1):
            has_slot = jnp.logical_and(n_groups >= 1, (n_groups - 1) % 2 == slot)
            pl.when(jnp.logical_and(is_size, has_slot))(functools.partial(tail, n_blk, slot))

    def two_pass_step(j, masked):
        s_all = scores(j, 1)
        for h in range(hp):
            s = s_all[h]
            if masked:
                s = jnp.where(key_l <= qry_l, s, NEG_INF)
            m_prev = m_ref[h]
            m_new = jnp.maximum(m_prev, jnp.max(s, axis=0, keepdims=True))
            m_ref[h] = m_new
            acc_ref[h] = jnp.exp2(m_prev - m_new) * acc_ref[h] + _dot(values(h, j, 1), jnp.exp2(s - m_new).astype(BF16))

    @pl.when(jnp.max(lag_ref[...]) > MAX_LAG)
    def _():
        acc_ref[...] = jnp.zeros_like(acc_ref)
        m_ref[...] = jnp.full_like(m_ref, NEG_INF)
        lax.fori_loop(0, qi, lambda j, c: (two_pass_step(j, False), c)[1], 0)
        two_pass_step(qi, True)

    for pair in range(hp // HEAD_PAIR):
        outs = []
        for h in (pair * HEAD_PAIR, pair * HEAD_PAIR + 1):
            a = acc_ref[h]
            outs.append(a[:MLA_V] / a[MLA_V:MLA_V + 1])
        o_t = jnp.concatenate(outs, axis=0)
        o_ref[:, pair * LANES:(pair + 1) * LANES] = o_t.T.astype(o_ref.dtype)


def _mla_call(qm, km, vmt, *, t, hp, wide):
    B, S, _ = qm.shape
    return pl.pallas_call(
        functools.partial(_mla_kernel, t=t, hp=hp, wide=wide),
        grid=(B, MLA_HEADS // hp, S // t),
        in_specs=[pl.BlockSpec((None, t, hp * LANES), lambda b, g, qi: (b, qi, g)),
                  pl.BlockSpec((None, S, hp * LANES), lambda b, g, qi: (b, 0, g)),
                  pl.BlockSpec((hp * MLA_V, S), lambda b, g, qi: (g, b))],
        out_specs=pl.BlockSpec((None, t, hp * MLA_V), lambda b, g, qi: (b, qi, g)),
        out_shape=jax.ShapeDtypeStruct((B, S, MLA_HEADS * MLA_V), BF16),
        scratch_shapes=[pltpu.VMEM((hp, MLA_V + ONES_ROWS, t), F32),
                        pltpu.VMEM((hp, 1, t), F32),
                        pltpu.VMEM((hp, LANES, t), BF16),
                        pltpu.VMEM((2, hp, wide * t, t), BF16),
                        pltpu.VMEM((2, hp, 1, t), F32),
                        pltpu.VMEM((hp, 1, t), F32)],
        compiler_params=pltpu.CompilerParams(dimension_semantics=("parallel", "parallel", "arbitrary"),
                                             vmem_limit_bytes=VMEM_LIMIT),
        name="mla_attn",
    )(qm, km, vmt)


def _sb_kernel(q_ref, k_ref, vt_ref, tri_ref, o_ref, acc_ref, carry_ref, qt_ref, a_ref, tot_ref,
               *, t, hp, wide, tiles):
    step_i = pl.program_id(2)
    acc_ref[...] = jnp.zeros_like(acc_ref)
    carry_ref[...] = jnp.zeros_like(carry_ref)
    q_lane = lax.broadcasted_iota(jnp.int32, (t, LANES), 1)
    for tile in range(tiles):
        for pair in range(hp // HEAD_PAIR):
            q = q_ref[tile * t:(tile + 1) * t, pair * LANES:(pair + 1) * LANES].astype(F32)
            qt_ref[tile, pair * HEAD_PAIR] = jnp.where(q_lane < SB_DIM, q, 0.0).T.astype(BF16)
            qt_ref[tile, pair * HEAD_PAIR + 1] = jnp.where(q_lane >= SB_DIM, q, 0.0).T.astype(BF16)
    key_l = lax.broadcasted_iota(jnp.int32, (t, t), 0)
    qry_l = lax.broadcasted_iota(jnp.int32, (t, t), 1)

    valid = key_l < qry_l

    def tile_qi(tile):
        return step_i * tiles + tile

    def key_offset(j):
        return pl.multiple_of(j * t, t)

    def scores(items):
        zs = {}
        for n, (tile, j) in enumerate(items):
            for h in range(hp):
                pair = h // HEAD_PAIR
                k_blk = k_ref[pl.ds(key_offset(j), t), pair * LANES:(pair + 1) * LANES]
                zs[n, h] = _dot(k_blk, qt_ref[tile, h])
        return zs

    def suffix(z, masked):
        sp = jnp.maximum(z, jnp.log2(1.0 + jnp.exp2(jnp.minimum(z, SOFTPLUS_CLAMP))))
        if masked:
            sp = jnp.where(valid, sp, 0.0)
        c = _dot(tri_ref[...], sp.astype(BF16))
        a = jnp.exp2(z - c)
        if masked:
            a = jnp.where(valid, a, 0.0)
        return a.astype(BF16), c[0:1, :]

    def weights_pv(tile, h, j, a, total):
        vt = vt_ref[h * SB_DIM:(h + 1) * SB_DIM, pl.ds(key_offset(j), t)]
        carry = carry_ref[tile, h]
        acc_ref[tile, h] += _dot(vt, a) * jnp.exp2(carry)
        carry_ref[tile, h] = carry - total

    def step(items, masked):
        zs = scores(items)
        mid = {nh: suffix(z, masked) for nh, z in zs.items()}
        for n, (tile, j) in enumerate(items):
            for h in range(hp):
                weights_pv(tile, h, j, *mid[n, h])

    step([(tile, tile_qi(tile)) for tile in range(tiles)], True)

    @pl.when(step_i >= 1)
    def _():
        step([(tile, tile_qi(tile) - 1) for tile in range(tiles)], False)

    if tiles > 1:
        @pl.when(step_i == 0)
        def _():
            step([(tile, tile_qi(tile) - 1) for tile in range(1, tiles)], False)

    def rest_of_tile(tile, c):
        qi = tile_qi(tile)

        def group_items(g):
            return [(tile, qi - 2 - g * wide - b) for b in range(wide)]

        def refill(g, zs):
            for (b, h), z in zs.items():
                a_ref[g % 2, b, h], tot_ref[g % 2, b, h] = suffix(z, False)

        def back(g):
            for b, (_, j) in enumerate(group_items(g)):
                for h in range(hp):
                    weights_pv(tile, h, j, a_ref[g % 2, b, h], tot_ref[g % 2, b, h])

        def alive():
            return (jnp.max(carry_ref[tile]) > EXP2_IS_ZERO_BELOW).astype(jnp.int32)

        n_rest = jnp.maximum(qi - 1, 0)
        n_wide = n_rest // wide
        live_0 = alive()

        @pl.when(jnp.logical_and(n_wide >= 1, live_0 == 1))
        def _():
            refill(0, scores(group_items(0)))

        def cond(state):
            g, live = state
            return jnp.logical_and(g < n_wide, live == 1)

        def body(state):
            g, _ = state
            zs = scores(group_items(g))
            back(g - 1)
            refill(g, zs)
            return g + 1, alive()

        g_end, live = lax.while_loop(cond, body, (jnp.int32(1), live_0))

        @pl.when(jnp.logical_and(n_wide >= 1, live == 1))
        def _():
            back(g_end - 1)

        for r in range(wide - 1):
            @pl.when(jnp.logical_and(n_rest - n_wide * wide > r, live == 1))
            def _():
                step([(tile, qi - 2 - n_wide * wide - r)], False)
        return c

    @pl.when(jnp.max(carry_ref[...]) > EXP2_IS_ZERO_BELOW)
    def _():
        lax.fori_loop(0, tiles, rest_of_tile, 0)

    for tile in range(tiles):
        for pair in range(hp // HEAD_PAIR):
            o_t = jnp.concatenate([acc_ref[tile, pair * HEAD_PAIR], acc_ref[tile, pair * HEAD_PAIR + 1]], axis=0)
            o_ref[tile * t:(tile + 1) * t, pair * LANES:(pair + 1) * LANES] = o_t.T.astype(o_ref.dtype)


def _sb_call(qs, ks, vst, *, t, hp, wide, tiles):
    B, S, _ = qs.shape
    r = lax.broadcasted_iota(jnp.int32, (t, t), 0)
    c = lax.broadcasted_iota(jnp.int32, (t, t), 1)
    tri = (c >= r).astype(BF16)
    w = hp * SB_DIM
    return pl.pallas_call(
        functools.partial(_sb_kernel, t=t, hp=hp, wide=wide, tiles=tiles),
        grid=(B, SB_HEADS // hp, S // (tiles * t)),
        in_specs=[pl.BlockSpec((None, tiles * t, w), lambda b, g, i: (b, i, g)),
                  pl.BlockSpec((None, S, w), lambda b, g, i: (b, 0, g)),
                  pl.BlockSpec((w, S), lambda b, g, i: (g, b)),
                  pl.BlockSpec((t, t), lambda b, g, i: (0, 0))],
        out_specs=pl.BlockSpec((None, tiles * t, w), lambda b, g, i: (b, i, g)),
        out_shape=jax.ShapeDtypeStruct((B, S, SB_HEADS * SB_DIM), BF16),
        scratch_shapes=[pltpu.VMEM((tiles, hp, SB_DIM, t), F32),
                        pltpu.VMEM((tiles, hp, 1, t), F32),
                        pltpu.VMEM((tiles, hp, LANES, t), BF16),
                        pltpu.VMEM((2, wide, hp, t, t), BF16),
                        pltpu.VMEM((2, wide, hp, 1, t), F32)],
        compiler_params=pltpu.CompilerParams(dimension_semantics=("parallel", "parallel", "arbitrary"),
                                             vmem_limit_bytes=VMEM_LIMIT),
        name="sb_attn",
    )(qs, ks, vst, tri)


def _post_kernel(x_ref, oa_ref, ob_ref, p_ref, g_mix_ref, wga_ref, wgb_ref, wbra_ref, wbrb_ref, wout_ref,
                 g_ffn_ref, wfg_ref, wfu_ref, wfd_ref, wpg_ref, wpp_ref, g_ple_ref, g_fin_ref, out_ref,
                 *, ff_chunk, final_norm):
    x = x_ref[...]
    n = _rms(x, g_mix_ref[...]).astype(BF16)
    merged = (jax.nn.sigmoid(_dot(n, wga_ref[...])) * _dot(oa_ref[...], wbra_ref[...])
              + jax.nn.sigmoid(_dot(n, wgb_ref[...])) * _dot(ob_ref[...], wbrb_ref[...]))
    h = x + _dot(merged.astype(BF16), wout_ref[...])

    n2 = _rms(h, g_ffn_ref[...]).astype(BF16)
    d_ff = wfg_ref.shape[1]
    ff = None
    for c0 in range(0, d_ff, ff_chunk):
        c1 = min(c0 + ff_chunk, d_ff)
        g = _dot(n2, wfg_ref[:, c0:c1])
        u = _dot(n2, wfu_ref[:, c0:c1])
        part = _dot((g * jax.nn.sigmoid(g) * u).astype(BF16), wfd_ref[c0:c1, :])
        ff = part if ff is None else ff + part
    h = h + ff

    e = _rms(_dot(p_ref[...].astype(BF16), wpp_ref[...]), g_ple_ref[...])
    h = h + jax.nn.sigmoid(_dot(h.astype(BF16), wpg_ref[...])) * e
    out_ref[...] = _rms(h, g_fin_ref[...]) if final_norm else h


def _post_call(x2, oa, ob, p2, consts, *, tm, ff_chunk, final_norm):
    T, D = x2.shape
    row = lambda w: pl.BlockSpec((tm, w), lambda i: (i, 0))
    return pl.pallas_call(
        functools.partial(_post_kernel, ff_chunk=ff_chunk, final_norm=final_norm),
        grid=(T // tm,),
        in_specs=[row(D), row(oa.shape[1]), row(ob.shape[1]), row(p2.shape[1])]
                 + [_const_spec(c.shape) for c in consts],
        out_specs=row(D),
        out_shape=jax.ShapeDtypeStruct((T, D), F32),
        compiler_params=pltpu.CompilerParams(dimension_semantics=("parallel",),
                                             vmem_limit_bytes=VMEM_LIMIT),
        name="post",
    )(x2, oa, ob, p2, *consts)


def _rotate_half_cols(w):
    half = w.shape[-1] // 2
    return jnp.concatenate([-w[..., half:], w[..., :half]], axis=-1)


def _layer_weights(w_in, w_q_b, w_kv_b):
    d = w_in.shape[0]
    o = 0
    cols = []
    for wd in (MLA_Q_RANK, MLA_KV_RANK, MLA_ROPE, SB_HEADS * SB_DIM, SB_HEADS * SB_DIM, SB_HEADS * SB_DIM, d, d):
        cols.append(w_in[:, o:o + wd])
        o += wd
    w_cq, w_ckv, w_kpe, w_qs, w_ks, w_vs, w_ga, w_gb = cols

    kpe_tile = jnp.concatenate([jnp.zeros_like(w_in[:, :MLA_NOPE]), w_kpe, _rotate_half_cols(w_kpe)], axis=1)
    wa = jnp.concatenate([w_cq, w_ckv, kpe_tile], axis=1)
    wsb = jnp.concatenate([w_qs, w_ks, w_vs], axis=1)

    qb = w_q_b.reshape(MLA_Q_RANK, MLA_HEADS, MLA_NOPE + MLA_ROPE)
    wq = jnp.concatenate([qb, _rotate_half_cols(qb[..., MLA_NOPE:])], axis=-1).reshape(MLA_Q_RANK, MLA_HEADS * LANES)

    kvb = w_kv_b.reshape(MLA_KV_RANK, MLA_HEADS, MLA_NOPE + MLA_V)
    wk = jnp.pad(kvb[..., :MLA_NOPE], ((0, 0), (0, 0), (0, LANES - MLA_NOPE))).reshape(MLA_KV_RANK, MLA_HEADS * LANES)
    wv = kvb[..., MLA_NOPE:].reshape(MLA_KV_RANK, MLA_HEADS * MLA_V)
    bf = lambda t: t.astype(BF16)
    return tuple(map(bf, (wa, wsb, wq, wk, wv, w_ga, w_gb)))


def _rope_freq_row():
    inv_freq = 1.0 / (ROPE_THETA ** (jnp.arange(0, MLA_ROPE, 2, dtype=F32) / MLA_ROPE))
    both = jnp.concatenate([inv_freq, inv_freq])
    return jnp.pad(both, (MLA_NOPE, LANES - MLA_NOPE - MLA_ROPE)).reshape(1, LANES)


def _tiles(B, S):
    T = B * S
    tm_proj = min(512, T)
    tm_post = min(512, T)
    t_sb = min(256, S)
    t_mla = min(512, S)
    heads_per_step = 8
    sb_wide = 2
    mla_wide = 1
    sb_tiles = 2
    return tm_proj, tm_post, t_sb, t_mla, heads_per_step, sb_wide, mla_wide, sb_tiles


def kernel(x, p, positions, g_mix, w_in, g_q_a, w_q_b, g_kv_a, w_kv_b, w_br_mla, w_br_sb, w_out, g_ffn,
           w_ffn_gate, w_ffn_up, w_ffn_down, w_ple_gate, w_ple_proj, g_ple, g_final):
    B, S, D = x.shape
    T = B * S
    depth = w_in.shape[0]
    tm_proj, tm_post, t_sb, t_mla, hp, sb_wide, mla_wide, sb_tiles = _tiles(B, S)
    d_ff = w_ffn_gate.shape[-1]
    ff_chunk = min(d_ff, 1024)
    row = lambda g: g.reshape(1, -1).astype(F32)
    bf = lambda t: t.astype(BF16)

    pos2 = positions.reshape(T, 1).astype(jnp.int32)
    freq = _rope_freq_row()
    h = x.reshape(T, D)
    for i in range(depth):
        wa, wsb, wq, wk, wv, w_ga, w_gb = _layer_weights(w_in[i], w_q_b[i], w_kv_b[i])
        qm, km, vm, qs, ks, vs = _proj_call(h, pos2, freq, row(g_mix[i]), wa, wsb, row(g_q_a[i]), wq,
                                            row(g_kv_a[i]), wk, wv, tm=tm_proj)
        shp = lambda t: t.reshape(B, S, t.shape[-1])
        o_a = _mla_call(shp(qm), shp(km), vm, t=t_mla, hp=hp, wide=mla_wide).reshape(T, -1)
        o_b = _sb_call(shp(qs), shp(ks), vs, t=t_sb, hp=hp, wide=sb_wide,
                       tiles=sb_tiles if (S // t_sb) % sb_tiles == 0 else 1).reshape(T, -1)
        consts = (row(g_mix[i]), w_ga, w_gb, bf(w_br_mla[i]), bf(w_br_sb[i]), bf(w_out[i]), row(g_ffn[i]),
                  bf(w_ffn_gate[i]), bf(w_ffn_up[i]), bf(w_ffn_down[i]), bf(w_ple_gate[i]), bf(w_ple_proj[i]),
                  row(g_ple[i]), row(g_final))
        h = _post_call(h, o_a, o_b, p[i].reshape(T, -1), consts, tm=tm_post, ff_chunk=ff_chunk,
                       final_norm=(i == depth - 1))
    return h.reshape(B, S, D)
```

```python
import functools
import math

import jax
import jax.numpy as jnp
from jax import lax
from jax.experimental import pallas as pl
from jax.experimental.pallas import tpu as pltpu

EPS = 1e-6
MLA_HEADS = 8
MLA_NOPE = 64
MLA_ROPE = 32
MLA_V = 64
MLA_Q_RANK = 384
MLA_KV_RANK = 256
ROPE_THETA = 10000.0
MLA_SCALE = 1.0 / math.sqrt(MLA_NOPE + MLA_ROPE)
SB_HEADS = 8
SB_DIM = 64
SB_SCALE = 1.0 / math.sqrt(SB_DIM)
NEG_INF = -1e30
LOG2E = math.log2(math.e)
SOFTPLUS_CLAMP = 64.0
EXP2_IS_ZERO_BELOW = -151.0

LANES = 128
HEAD_PAIR = 2
VMEM_LIMIT = 56 * 1024 * 1024

F32 = jnp.float32
BF16 = jnp.bfloat16


def _rms(x, g):
    return x * lax.rsqrt(jnp.mean(x * x, axis=-1, keepdims=True) + EPS) * g


def _dot(a, b):
    return jnp.dot(a, b, preferred_element_type=F32)


def _dot_nt(a, b):
    return lax.dot_general(a, b, (((1,), (1,)), ((), ())), preferred_element_type=F32)


def _proj_kernel(x_ref, pos_ref, freq_ref, g_mix_ref, wa_ref, wsb_ref, gq_ref, wq_ref,
                 gkv_ref, wk_ref, wv_ref,
                 qm_ref, km_ref, vm_ref, qs_ref, ks_ref, vs_ref):
    n = _rms(x_ref[...], g_mix_ref[...]).astype(BF16)
    pa = _dot(n, wa_ref[...])
    sb = _dot(n, wsb_ref[...])
    c_q = pa[:, :MLA_Q_RANK]
    c_kv = pa[:, MLA_Q_RANK:MLA_Q_RANK + MLA_KV_RANK]
    kpe = pa[:, MLA_Q_RANK + MLA_KV_RANK:]

    ang = pos_ref[...].astype(F32) * freq_ref[...]
    lane = lax.broadcasted_iota(jnp.int32, ang.shape, 1)
    cos_t = jnp.where(lane < MLA_NOPE + MLA_ROPE, jnp.cos(ang), 0.0)
    sin_t = jnp.sin(ang)

    def rope(tile):
        return tile * cos_t + pltpu.roll(tile, LANES - MLA_ROPE, 1) * sin_t

    nq = _rms(c_q, gq_ref[...]).astype(BF16)
    q = _dot(nq, wq_ref[...])
    nkv = _rms(c_kv, gkv_ref[...]).astype(BF16)
    kn = _dot(nkv, wk_ref[...])
    k_pe = rope(kpe)
    for h in range(MLA_HEADS):
        sl = slice(h * LANES, (h + 1) * LANES)
        qm_ref[:, sl] = (rope(q[:, sl]) * (MLA_SCALE * LOG2E)).astype(BF16)
        km_ref[:, sl] = (kn[:, sl] + k_pe).astype(BF16)
    vm_ref[...] = _dot(nkv, wv_ref[...]).astype(BF16).T

    w = SB_HEADS * SB_DIM
    qs_ref[...] = (sb[:, :w] * (SB_SCALE * LOG2E)).astype(BF16)
    ks_ref[...] = sb[:, w:2 * w].astype(BF16)
    vs_ref[...] = sb[:, 2 * w:].astype(BF16).T


def _const_spec(shape):
    return pl.BlockSpec(shape, lambda *_: (0,) * len(shape), pipeline_mode=pl.Buffered(1))


def _proj_call(x2, pos2, freq, g_mix, wa, wsb, gq, wq, gkv, wk, wv, *, tm):
    T, D = x2.shape
    row = lambda w: pl.BlockSpec((tm, w), lambda i: (i, 0))
    consts = (freq, g_mix, wa, wsb, gq, wq, gkv, wk, wv)
    col = lambda w: pl.BlockSpec((w, tm), lambda i: (0, i))
    w_mla, w_v, w_sb = MLA_HEADS * LANES, MLA_HEADS * MLA_V, SB_HEADS * SB_DIM
    tok = lambda w: jax.ShapeDtypeStruct((T, w), BF16)
    tok_t = lambda w: jax.ShapeDtypeStruct((w, T), BF16)
    return pl.pallas_call(
        _proj_kernel,
        grid=(T // tm,),
        in_specs=[row(D), row(1)] + [_const_spec(c.shape) for c in consts],
        out_specs=[row(w_mla), row(w_mla), col(w_v), row(w_sb), row(w_sb), col(w_sb)],
        out_shape=[tok(w_mla), tok(w_mla), tok_t(w_v), tok(w_sb), tok(w_sb), tok_t(w_sb)],
        compiler_params=pltpu.CompilerParams(dimension_semantics=("parallel",),
                                             vmem_limit_bytes=VMEM_LIMIT),
        name="proj",
    )(x2, pos2, *consts)


ONES_ROWS = 16
MAX_LAG = 64.0


def _mla_kernel(q_ref, k_ref, vt_ref, o_ref, acc_ref, m_ref, qt_ref, p_ref, scale_ref, lag_ref, *, t, hp, wide):
    qi = pl.program_id(2)
    for h in range(hp):
        qt_ref[h] = q_ref[:, h * LANES:(h + 1) * LANES].astype(F32).T.astype(BF16)
    key_l = lax.broadcasted_iota(jnp.int32, (t, t), 0)
    qry_l = lax.broadcasted_iota(jnp.int32, (t, t), 1)

    def scores(j, n_blk):
        k0 = pl.multiple_of(j * t, t)
        return [_dot(k_ref[pl.ds(k0, n_blk * t), h * LANES:(h + 1) * LANES], qt_ref[h]) for h in range(hp)]

    def values(h, j, n_blk):
        k0 = pl.multiple_of(j * t, t)
        w = n_blk * t
        return jnp.concatenate([vt_ref[h * MLA_V:(h + 1) * MLA_V, pl.ds(k0, w)],
                                jnp.ones((ONES_ROWS, w), BF16)], axis=0)

    def one_pass(h, s, diag_last, first):
        if diag_last:
            n_keys = s.shape[0]
            key_g = lax.broadcasted_iota(jnp.int32, (n_keys, t), 0)
            qry_g = lax.broadcasted_iota(jnp.int32, (n_keys, t), 1) + (n_keys - t)
            s = jnp.where(key_g <= qry_g, s, NEG_INF)
        m_old = s[0:1, :] if first else m_ref[h]
        p = jnp.exp2(s - m_old).astype(BF16)
        lag = jnp.maximum(jnp.max(s, axis=0, keepdims=True) - m_old, 0.0)
        m_ref[h] = m_old + lag
        lag_ref[h] = jnp.maximum(lag_ref[h], lag)
        return p, jnp.exp2(-lag)

    def add_pv(h, j, n_blk, p, scale):
        acc_ref[h] = (acc_ref[h] + _dot(values(h, j, n_blk), p)) * scale

    def refill(g, slot, s_all, first):
        for h in range(hp):
            p_ref[slot, h], scale_ref[slot, h] = one_pass(h, s_all[h], False, first)

    def back(g, slot):
        for h in range(hp):
            add_pv(h, g * wide, wide, p_ref[slot, h], scale_ref[slot, h])

    def pipelined(g, slot):
        s_all = scores(g * wide, wide)
        back(g - 1, 1 - slot)
        refill(g, slot, s_all, False)

    def body(i, c):
        pipelined(2 * i + 1, 1)
        pipelined(2 * i + 2, 0)
        return c

    def tail(n_blk, pending_slot):
        j = qi + 1 - n_blk
        s_all = scores(j, n_blk)
        if pending_slot is not None:
            back(n_groups - 1, pending_slot)
        for h in range(hp):
            add_pv(h, j, n_blk, *one_pass(h, s_all[h], True, pending_slot is None))

    acc_ref[...] = jnp.zeros_like(acc_ref)
    lag_ref[...] = jnp.zeros_like(lag_ref)
    n_groups = qi // wide
    n_pairs = jnp.maximum(n_groups - 1, 0) // 2

    @pl.when(n_groups >= 1)
    def _():
        refill(0, 0, scores(0, wide), True)

    lax.fori_loop(0, n_pairs, body, 0)

    @pl.when(jnp.logical_and(n_groups >= 2, n_groups % 2 == 0))
    def _():
        pipelined(n_groups - 1, 1)

    for n_blk in range(1, wide + 1):
        is_size = qi % wide == n_blk - 1
        pl.when(jnp.logical_and(is_size, n_groups == 0))(functools.partial(tail, n_blk, None))
        for slot in (0, 1):
            has_slot = jnp.logical_and(n_groups >= 1, (n_groups - 1) % 2 == slot)
            pl.when(jnp.logical_and(is_size, has_slot))(functools.partial(tail, n_blk, slot))

    def two_pass_step(j, masked):
        s_all = scores(j, 1)
        for h in range(hp):
            s = s_all[h]
            if masked:
                s = jnp.where(key_l <= qry_l, s, NEG_INF)
            m_prev = m_ref[h]
            m_new = jnp.maximum(m_prev, jnp.max(s, axis=0, keepdims=True))
            m_ref[h] = m_new
            acc_ref[h] = jnp.exp2(m_prev - m_new) * acc_ref[h] + _dot(values(h, j, 1), jnp.exp2(s - m_new).astype(BF16))

    @pl.when(jnp.max(lag_ref[...]) > MAX_LAG)
    def _():
        acc_ref[...] = jnp.zeros_like(acc_ref)
        m_ref[...] = jnp.full_like(m_ref, NEG_INF)
        lax.fori_loop(0, qi, lambda j, c: (two_pass_step(j, False), c)[1], 0)
        two_pass_step(qi, True)

    for pair in range(hp // HEAD_PAIR):
        outs = []
        for h in (pair * HEAD_PAIR, pair * HEAD_PAIR + 1):
            a = acc_ref[h]
            outs.append(a[:MLA_V] / a[MLA_V:MLA_V + 1])
        o_t = jnp.concatenate(outs, axis=0)
        o_ref[:, pair * LANES:(pair + 1) * LANES] = o_t.T.astype(o_ref.dtype)


def _mla_call(qm, km, vmt, *, t, hp, wide):
    B, S, _ = qm.shape
    return pl.pallas_call(
        functools.partial(_mla_kernel, t=t, hp=hp, wide=wide),
        grid=(B, MLA_HEADS // hp, S // t),
        in_specs=[pl.BlockSpec((None, t, hp * LANES), lambda b, g, qi: (b, qi, g)),
                  pl.BlockSpec((None, S, hp * LANES), lambda b, g, qi: (b, 0, g)),
                  pl.BlockSpec((hp * MLA_V, S), lambda b, g, qi: (g, b))],
        out_specs=pl.BlockSpec((None, t, hp * MLA_V), lambda b, g, qi: (b, qi, g)),
        out_shape=jax.ShapeDtypeStruct((B, S, MLA_HEADS * MLA_V), BF16),
        scratch_shapes=[pltpu.VMEM((hp, MLA_V + ONES_ROWS, t), F32),
                        pltpu.VMEM((hp, 1, t), F32),
                        pltpu.VMEM((hp, LANES, t), BF16),
                        pltpu.VMEM((2, hp, wide * t, t), BF16),
                        pltpu.VMEM((2, hp, 1, t), F32),
                        pltpu.VMEM((hp, 1, t), F32)],
        compiler_params=pltpu.CompilerParams(dimension_semantics=("parallel", "parallel", "arbitrary"),
                                             vmem_limit_bytes=VMEM_LIMIT),
        name="mla_attn",
    )(qm, km, vmt)


def _sb_kernel(q_ref, k_ref, vt_ref, tri_ref, o_ref, acc_ref, carry_ref, qt_ref, a_ref, tot_ref,
               *, t, hp, wide, tiles):
    step_i = pl.program_id(2)
    acc_ref[...] = jnp.zeros_like(acc_ref)
    carry_ref[...] = jnp.zeros_like(carry_ref)
    q_lane = lax.broadcasted_iota(jnp.int32, (t, LANES), 1)
    for tile in range(tiles):
        for pair in range(hp // HEAD_PAIR):
            q = q_ref[tile * t:(tile + 1) * t, pair * LANES:(pair + 1) * LANES].astype(F32)
            qt_ref[tile, pair * HEAD_PAIR] = jnp.where(q_lane < SB_DIM, q, 0.0).T.astype(BF16)
            qt_ref[tile, pair * HEAD_PAIR + 1] = jnp.where(q_lane >= SB_DIM, q, 0.0).T.astype(BF16)
    key_l = lax.broadcasted_iota(jnp.int32, (t, t), 0)
    qry_l = lax.broadcasted_iota(jnp.int32, (t, t), 1)

    valid = key_l < qry_l

    def tile_qi(tile):
        return step_i * tiles + tile

    def key_offset(j):
        return pl.multiple_of(j * t, t)

    def scores(items):
        zs = {}
        for n, (tile, j) in enumerate(items):
            for h in range(hp):
                pair = h // HEAD_PAIR
                k_blk = k_ref[pl.ds(key_offset(j), t), pair * LANES:(pair + 1) * LANES]
                zs[n, h] = _dot(k_blk, qt_ref[tile, h])
        return zs

    def suffix(z, masked):
        sp = jnp.maximum(z, jnp.log2(1.0 + jnp.exp2(jnp.minimum(z, SOFTPLUS_CLAMP))))
        if masked:
            sp = jnp.where(valid, sp, 0.0)
        c = _dot(tri_ref[...], sp.astype(BF16))
        a = jnp.exp2(z - c)
        if masked:
            a = jnp.where(valid, a, 0.0)
        return a.astype(BF16), c[0:1, :]

    def weights_pv(tile, h, j, a, total):
        vt = vt_ref[h * SB_DIM:(h + 1) * SB_DIM, pl.ds(key_offset(j), t)]
        carry = carry_ref[tile, h]
        acc_ref[tile, h] += _dot(vt, a) * jnp.exp2(carry)
        carry_ref[tile, h] = carry - total

    def step(items, masked):
        zs = scores(items)
        mid = {nh: suffix(z, masked) for nh, z in zs.items()}
        for n, (tile, j) in enumerate(items):
            for h in range(hp):
                weights_pv(tile, h, j, *mid[n, h])

    step([(tile, tile_qi(tile)) for tile in range(tiles)], True)

    @pl.when(step_i >= 1)
    def _():
        step([(tile, tile_qi(tile) - 1) for tile in range(tiles)], False)

    if tiles > 1:
        @pl.when(step_i == 0)
        def _():
            step([(tile, tile_qi(tile) - 1) for tile in range(1, tiles)], False)

    def rest_of_tile(tile, c):
        qi = tile_qi(tile)

        def group_items(g):
            return [(tile, qi - 2 - g * wide - b) for b in range(wide)]

        def refill(g, zs):
            for (b, h), z in zs.items():
                a_ref[g % 2, b, h], tot_ref[g % 2, b, h] = suffix(z, False)

        def back(g):
            for b, (_, j) in enumerate(group_items(g)):
                for h in range(hp):
                    weights_pv(tile, h, j, a_ref[g % 2, b, h], tot_ref[g % 2, b, h])

        def alive():
            return (jnp.max(carry_ref[tile]) > EXP2_IS_ZERO_BELOW).astype(jnp.int32)

        n_rest = jnp.maximum(qi - 1, 0)
        n_wide = n_rest // wide
        live_0 = alive()

        @pl.when(jnp.logical_and(n_wide >= 1, live_0 == 1))
        def _():
            refill(0, scores(group_items(0)))

        def cond(state):
            g, live = state
            return jnp.logical_and(g < n_wide, live == 1)

        def body(state):
            g, _ = state
            zs = scores(group_items(g))
            back(g - 1)
            refill(g, zs)
            return g + 1, alive()

        g_end, live = lax.while_loop(cond, body, (jnp.int32(1), live_0))

        @pl.when(jnp.logical_and(n_wide >= 1, live == 1))
        def _():
            back(g_end - 1)

        for r in range(wide - 1):
            @pl.when(jnp.logical_and(n_rest - n_wide * wide > r, live == 1))
            def _():
                step([(tile, qi - 2 - n_wide * wide - r)], False)
        return c

    @pl.when(jnp.max(carry_ref[...]) > EXP2_IS_ZERO_BELOW)
    def _():
        lax.fori_loop(0, tiles, rest_of_tile, 0)

    for tile in range(tiles):
        for pair in range(hp // HEAD_PAIR):
            o_t = jnp.concatenate([acc_ref[tile, pair * HEAD_PAIR], acc_ref[tile, pair * HEAD_PAIR + 1]], axis=0)
            o_ref[tile * t:(tile + 1) * t, pair * LANES:(pair + 1) * LANES] = o_t.T.astype(o_ref.dtype)


def _sb_call(qs, ks, vst, *, t, hp, wide, tiles):
    B, S, _ = qs.shape
    r = lax.broadcasted_iota(jnp.int32, (t, t), 0)
    c = lax.broadcasted_iota(jnp.int32, (t, t), 1)
    tri = (c >= r).astype(BF16)
    w = hp * SB_DIM
    return pl.pallas_call(
        functools.partial(_sb_kernel, t=t, hp=hp, wide=wide, tiles=tiles),
        grid=(B, SB_HEADS // hp, S // (tiles * t)),
        in_specs=[pl.BlockSpec((None, tiles * t, w), lambda b, g, i: (b, i, g)),
                  pl.BlockSpec((None, S, w), lambda b, g, i: (b, 0, g)),
                  pl.BlockSpec((w, S), lambda b, g, i: (g, b)),
                  pl.BlockSpec((t, t), lambda b, g, i: (0, 0))],
        out_specs=pl.BlockSpec((None, tiles * t, w), lambda b, g, i: (b, i, g)),
        out_shape=jax.ShapeDtypeStruct((B, S, SB_HEADS * SB_DIM), BF16),
        scratch_shapes=[pltpu.VMEM((tiles, hp, SB_DIM, t), F32),
                        pltpu.VMEM((tiles, hp, 1, t), F32),
                        pltpu.VMEM((tiles, hp, LANES, t), BF16),
                        pltpu.VMEM((2, wide, hp, t, t), BF16),
                        pltpu.VMEM((2, wide, hp, 1, t), F32)],
        compiler_params=pltpu.CompilerParams(dimension_semantics=("parallel", "parallel", "arbitrary"),
                                             vmem_limit_bytes=VMEM_LIMIT),
        name="sb_attn",
    )(qs, ks, vst, tri)


def _post_kernel(x_ref, oa_ref, ob_ref, p_ref, g_mix_ref, wga_ref, wgb_ref, wbra_ref, wbrb_ref, wout_ref,
                 g_ffn_ref, wfg_ref, wfu_ref, wfd_ref, wpg_ref, wpp_ref, g_ple_ref, g_fin_ref, out_ref,
                 *, ff_chunk, final_norm):
    x = x_ref[...]
    n = _rms(x, g_mix_ref[...]).astype(BF16)
    merged = (jax.nn.sigmoid(_dot(n, wga_ref[...])) * _dot(oa_ref[...], wbra_ref[...])
              + jax.nn.sigmoid(_dot(n, wgb_ref[...])) * _dot(ob_ref[...], wbrb_ref[...]))
    h = x + _dot(merged.astype(BF16), wout_ref[...])

    n2 = _rms(h, g_ffn_ref[...]).astype(BF16)
    d_ff = wfg_ref.shape[1]
    ff = None
    for c0 in range(0, d_ff, ff_chunk):
        c1 = min(c0 + ff_chunk, d_ff)
        g = _dot(n2, wfg_ref[:, c0:c1])
        u = _dot(n2, wfu_ref[:, c0:c1])
        part = _dot((g * jax.nn.sigmoid(g) * u).astype(BF16), wfd_ref[c0:c1, :])
        ff = part if ff is None else ff + part
    h = h + ff

    e = _rms(_dot(p_ref[...].astype(BF16), wpp_ref[...]), g_ple_ref[...])
    h = h + jax.nn.sigmoid(_dot(h.astype(BF16), wpg_ref[...])) * e
    out_ref[...] = _rms(h, g_fin_ref[...]) if final_norm else h


def _post_call(x2, oa, ob, p2, consts, *, tm, ff_chunk, final_norm):
    T, D = x2.shape
    row = lambda w: pl.BlockSpec((tm, w), lambda i: (i, 0))
    return pl.pallas_call(
        functools.partial(_post_kernel, ff_chunk=ff_chunk, final_norm=final_norm),
        grid=(T // tm,),
        in_specs=[row(D), row(oa.shape[1]), row(ob.shape[1]), row(p2.shape[1])]
                 + [_const_spec(c.shape) for c in consts],
        out_specs=row(D),
        out_shape=jax.ShapeDtypeStruct((T, D), F32),
        compiler_params=pltpu.CompilerParams(dimension_semantics=("parallel",),
                                             vmem_limit_bytes=VMEM_LIMIT),
        name="post",
    )(x2, oa, ob, p2, *consts)


def _rotate_half_cols(w):
    half = w.shape[-1] // 2
    return jnp.concatenate([-w[..., half:], w[..., :half]], axis=-1)


def _layer_weights(w_in, w_q_b, w_kv_b):
    d = w_in.shape[0]
    o = 0
    cols = []
    for wd in (MLA_Q_RANK, MLA_KV_RANK, MLA_ROPE, SB_HEADS * SB_DIM, SB_HEADS * SB_DIM, SB_HEADS * SB_DIM, d, d):
        cols.append(w_in[:, o:o + wd])
        o += wd
    w_cq, w_ckv, w_kpe, w_qs, w_ks, w_vs, w_ga, w_gb = cols

    kpe_tile = jnp.concatenate([jnp.zeros_like(w_in[:, :MLA_NOPE]), w_kpe, _rotate_half_cols(w_kpe)], axis=1)
    wa = jnp.concatenate([w_cq, w_ckv, kpe_tile], axis=1)
    wsb = jnp.concatenate([w_qs, w_ks, w_vs], axis=1)

    qb = w_q_b.reshape(MLA_Q_RANK, MLA_HEADS, MLA_NOPE + MLA_ROPE)
    wq = jnp.concatenate([qb, _rotate_half_cols(qb[..., MLA_NOPE:])], axis=-1).reshape(MLA_Q_RANK, MLA_HEADS * LANES)

    kvb = w_kv_b.reshape(MLA_KV_RANK, MLA_HEADS, MLA_NOPE + MLA_V)
    wk = jnp.pad(kvb[..., :MLA_NOPE], ((0, 0), (0, 0), (0, LANES - MLA_NOPE))).reshape(MLA_KV_RANK, MLA_HEADS * LANES)
    wv = kvb[..., MLA_NOPE:].reshape(MLA_KV_RANK, MLA_HEADS * MLA_V)
    bf = lambda t: t.astype(BF16)
    return tuple(map(bf, (wa, wsb, wq, wk, wv, w_ga, w_gb)))


def _rope_freq_row():
    inv_freq = 1.0 / (ROPE_THETA ** (jnp.arange(0, MLA_ROPE, 2, dtype=F32) / MLA_ROPE))
    both = jnp.concatenate([inv_freq, inv_freq])
    return jnp.pad(both, (MLA_NOPE, LANES - MLA_NOPE - MLA_ROPE)).reshape(1, LANES)


def _tiles(B, S):
    T = B * S
    tm_proj = min(1024, T)
    tm_post = min(512, T)
    t_sb = min(256, S)
    t_mla = min(512, S)
    heads_per_step = 8
    sb_wide = 2
    mla_wide = 1
    sb_tiles = 2
    return tm_proj, tm_post, t_sb, t_mla, heads_per_step, sb_wide, mla_wide, sb_tiles


def kernel(x, p, positions, g_mix, w_in, g_q_a, w_q_b, g_kv_a, w_kv_b, w_br_mla, w_br_sb, w_out, g_ffn,
           w_ffn_gate, w_ffn_up, w_ffn_down, w_ple_gate, w_ple_proj, g_ple, g_final):
    B, S, D = x.shape
    T = B * S
    depth = w_in.shape[0]
    tm_proj, tm_post, t_sb, t_mla, hp, sb_wide, mla_wide, sb_tiles = _tiles(B, S)
    d_ff = w_ffn_gate.shape[-1]
    ff_chunk = min(d_ff, 1024)
    row = lambda g: g.reshape(1, -1).astype(F32)
    bf = lambda t: t.astype(BF16)

    pos2 = positions.reshape(T, 1).astype(jnp.int32)
    freq = _rope_freq_row()
    h = x.reshape(T, D)
    for i in range(depth):
        wa, wsb, wq, wk, wv, w_ga, w_gb = _layer_weights(w_in[i], w_q_b[i], w_kv_b[i])
        qm, km, vm, qs, ks, vs = _proj_call(h, pos2, freq, row(g_mix[i]), wa, wsb, row(g_q_a[i]), wq,
                                            row(g_kv_a[i]), wk, wv, tm=tm_proj)
        shp = lambda t: t.reshape(B, S, t.shape[-1])
        o_a = _mla_call(shp(qm), shp(km), vm, t=t_mla, hp=hp, wide=mla_wide).reshape(T, -1)
        o_b = _sb_call(shp(qs), shp(ks), vs, t=t_sb, hp=hp, wide=sb_wide,
                       tiles=sb_tiles if (S // t_sb) % sb_tiles == 0 else 1).reshape(T, -1)
        consts = (row(g_mix[i]), w_ga, w_gb, bf(w_br_mla[i]), bf(w_br_sb[i]), bf(w_out[i]), row(g_ffn[i]),
                  bf(w_ffn_gate[i]), bf(w_ffn_up[i]), bf(w_ffn_down[i]), bf(w_ple_gate[i]), bf(w_ple_proj[i]),
                  row(g_ple[i]), row(g_final))
        h = _post_call(h, o_a, o_b, p[i].reshape(T, -1), consts, tm=tm_post, ff_chunk=ff_chunk,
                       final_norm=(i == depth - 1))
    return h.reshape(B, S, D)
```

```python
import functools
import math

import jax
import jax.numpy as jnp
from jax import lax
from jax.experimental import pallas as pl
from jax.experimental.pallas import tpu as pltpu

EPS = 1e-6
MLA_HEADS = 8
MLA_NOPE = 64
MLA_ROPE = 32
MLA_V = 64
MLA_Q_RANK = 384
MLA_KV_RANK = 256
ROPE_THETA = 10000.0
MLA_SCALE = 1.0 / math.sqrt(MLA_NOPE + MLA_ROPE)
SB_HEADS = 8
SB_DIM = 64
SB_SCALE = 1.0 / math.sqrt(SB_DIM)
NEG_INF = -1e30
LOG2E = math.log2(math.e)
SOFTPLUS_CLAMP = 64.0
EXP2_IS_ZERO_BELOW = -151.0

LANES = 128
HEAD_PAIR = 2
VMEM_LIMIT = 56 * 1024 * 1024

F32 = jnp.float32
BF16 = jnp.bfloat16


def _rms(x, g):
    return x * lax.rsqrt(jnp.mean(x * x, axis=-1, keepdims=True) + EPS) * g


def _dot(a, b):
    return jnp.dot(a, b, preferred_element_type=F32)


def _proj_kernel(x_ref, pos_ref, freq_ref, g_mix_ref, wa_ref, wsb_ref, gq_ref, wq_ref,
                 gkv_ref, wk_ref, wv_ref,
                 qm_ref, km_ref, vm_ref, qs_ref, ks_ref, vs_ref):
    n = _rms(x_ref[...], g_mix_ref[...]).astype(BF16)
    pa = _dot(n, wa_ref[...])
    sb = _dot(n, wsb_ref[...])
    c_q = pa[:, :MLA_Q_RANK]
    c_kv = pa[:, MLA_Q_RANK:MLA_Q_RANK + MLA_KV_RANK]
    kpe = pa[:, MLA_Q_RANK + MLA_KV_RANK:]

    ang = pos_ref[...].astype(F32) * freq_ref[...]
    lane = lax.broadcasted_iota(jnp.int32, ang.shape, 1)
    cos_t = jnp.where(lane < MLA_NOPE + MLA_ROPE, jnp.cos(ang), 0.0)
    sin_t = jnp.sin(ang)

    def rope(tile):
        return tile * cos_t + pltpu.roll(tile, LANES - MLA_ROPE, 1) * sin_t

    nq = _rms(c_q, gq_ref[...]).astype(BF16)
    q = _dot(nq, wq_ref[...])
    nkv = _rms(c_kv, gkv_ref[...]).astype(BF16)
    kn = _dot(nkv, wk_ref[...])
    k_pe = rope(kpe)
    for h in range(MLA_HEADS):
        sl = slice(h * LANES, (h + 1) * LANES)
        qm_ref[:, sl] = (rope(q[:, sl]) * (MLA_SCALE * LOG2E)).astype(BF16)
        km_ref[:, sl] = (kn[:, sl] + k_pe).astype(BF16)
    vm_ref[...] = _dot(nkv, wv_ref[...]).astype(BF16).T

    w = SB_HEADS * SB_DIM
    qs_ref[...] = (sb[:, :w] * (SB_SCALE * LOG2E)).astype(BF16)
    ks_ref[...] = sb[:, w:2 * w].astype(BF16)
    vs_ref[...] = sb[:, 2 * w:].astype(BF16).T


def _const_spec(shape):
    return pl.BlockSpec(shape, lambda *_: (0,) * len(shape), pipeline_mode=pl.Buffered(1))


def _proj_call(x2, pos2, freq, g_mix, wa, wsb, gq, wq, gkv, wk, wv, *, tm):
    T, D = x2.shape
    row = lambda w: pl.BlockSpec((tm, w), lambda i: (i, 0))
    consts = (freq, g_mix, wa, wsb, gq, wq, gkv, wk, wv)
    col = lambda w: pl.BlockSpec((w, tm), lambda i: (0, i))
    w_mla, w_v, w_sb = MLA_HEADS * LANES, MLA_HEADS * MLA_V, SB_HEADS * SB_DIM
    tok = lambda w: jax.ShapeDtypeStruct((T, w), BF16)
    tok_t = lambda w: jax.ShapeDtypeStruct((w, T), BF16)
    return pl.pallas_call(
        _proj_kernel,
        grid=(T // tm,),
        in_specs=[row(D), row(1)] + [_const_spec(c.shape) for c in consts],
        out_specs=[row(w_mla), row(w_mla), col(w_v), row(w_sb), row(w_sb), col(w_sb)],
        out_shape=[tok(w_mla), tok(w_mla), tok_t(w_v), tok(w_sb), tok(w_sb), tok_t(w_sb)],
        compiler_params=pltpu.CompilerParams(dimension_semantics=("parallel",),
                                             vmem_limit_bytes=VMEM_LIMIT),
        name="proj",
    )(x2, pos2, *consts)


ONES_ROWS = 16
MAX_LAG = 64.0


def _mla_kernel(q_ref, k_ref, vt_ref, o_ref, acc_ref, m_ref, qt_ref, p_ref, scale_ref, lag_ref, *, t, hp, wide):
    qi = pl.program_id(2)
    for h in range(hp):
        qt_ref[h] = q_ref[:, h * LANES:(h + 1) * LANES].astype(F32).T.astype(BF16)
    key_l = lax.broadcasted_iota(jnp.int32, (t, t), 0)
    qry_l = lax.broadcasted_iota(jnp.int32, (t, t), 1)

    def scores(j, n_blk):
        k0 = pl.multiple_of(j * t, t)
        return [_dot(k_ref[pl.ds(k0, n_blk * t), h * LANES:(h + 1) * LANES], qt_ref[h]) for h in range(hp)]

    def values(h, j, n_blk):
        k0 = pl.multiple_of(j * t, t)
        w = n_blk * t
        return jnp.concatenate([vt_ref[h * MLA_V:(h + 1) * MLA_V, pl.ds(k0, w)],
                                jnp.ones((ONES_ROWS, w), BF16)], axis=0)

    def one_pass(h, s, diag_last, first):
        if diag_last:
            n_keys = s.shape[0]
            key_g = lax.broadcasted_iota(jnp.int32, (n_keys, t), 0)
            qry_g = lax.broadcasted_iota(jnp.int32, (n_keys, t), 1) + (n_keys - t)
            s = jnp.where(key_g <= qry_g, s, NEG_INF)
        m_old = s[0:1, :] if first else m_ref[h]
        p = jnp.exp2(s - m_old).astype(BF16)
        lag = jnp.maximum(jnp.max(s, axis=0, keepdims=True) - m_old, 0.0)
        m_ref[h] = m_old + lag
        lag_ref[h] = jnp.maximum(lag_ref[h], lag)
        return p, jnp.exp2(-lag)

    def add_pv(h, j, n_blk, p, scale):
        acc_ref[h] = (acc_ref[h] + _dot(values(h, j, n_blk), p)) * scale

    def refill(g, slot, s_all, first):
        for h in range(hp):
            p_ref[slot, h], scale_ref[slot, h] = one_pass(h, s_all[h], False, first)

    def back(g, slot):
        for h in range(hp):
            add_pv(h, g * wide, wide, p_ref[slot, h], scale_ref[slot, h])

    def pipelined(g, slot):
        s_all = scores(g * wide, wide)
        back(g - 1, 1 - slot)
        refill(g, slot, s_all, False)

    def body(i, c):
        pipelined(2 * i + 1, 1)
        pipelined(2 * i + 2, 0)
        return c

    def tail(n_blk, pending_slot):
        j = qi + 1 - n_blk
        s_all = scores(j, n_blk)
        if pending_slot is not None:
            back(n_groups - 1, pending_slot)
        for h in range(hp):
            add_pv(h, j, n_blk, *one_pass(h, s_all[h], True, pending_slot is None))

    acc_ref[...] = jnp.zeros_like(acc_ref)
    lag_ref[...] = jnp.zeros_like(lag_ref)
    n_groups = qi // wide
    n_pairs = jnp.maximum(n_groups - 1, 0) // 2

    @pl.when(n_groups >= 1)
    def _():
        refill(0, 0, scores(0, wide), True)

    lax.fori_loop(0, n_pairs, body, 0)

    @pl.when(jnp.logical_and(n_groups >= 2, n_groups % 2 == 0))
    def _():
        pipelined(n_groups - 1, 1)

    for n_blk in range(1, wide + 1):
        is_size = qi % wide == n_blk - 1
        pl.when(jnp.logical_and(is_size, n_groups == 0))(functools.partial(tail, n_blk, None))
        for slot in (0, 1):
            has_slot = jnp.logical_and(n_groups >= 1, (n_groups - 1) % 2 == slot)
            pl.when(jnp.logical_and(is_size, has_slot))(functools.partial(tail, n_blk, slot))

    def two_pass_step(j, masked):
        s_all = scores(j, 1)
        for h in range(hp):
            s = s_all[h]
            if masked:
                s = jnp.where(key_l <= qry_l, s, NEG_INF)
            m_prev = m_ref[h]
            m_new = jnp.maximum(m_prev, jnp.max(s, axis=0, keepdims=True))
            m_ref[h] = m_new
            acc_ref[h] = jnp.exp2(m_prev - m_new) * acc_ref[h] + _dot(values(h, j, 1), jnp.exp2(s - m_new).astype(BF16))

    @pl.when(jnp.max(lag_ref[...]) > MAX_LAG)
    def _():
        acc_ref[...] = jnp.zeros_like(acc_ref)
        m_ref[...] = jnp.full_like(m_ref, NEG_INF)
        lax.fori_loop(0, qi, lambda j, c: (two_pass_step(j, False), c)[1], 0)
        two_pass_step(qi, True)

    for pair in range(hp // HEAD_PAIR):
        outs = []
        for h in (pair * HEAD_PAIR, pair * HEAD_PAIR + 1):
            a = acc_ref[h]
            outs.append(a[:MLA_V] / a[MLA_V:MLA_V + 1])
        o_t = jnp.concatenate(outs, axis=0)
        o_ref[:, pair * LANES:(pair + 1) * LANES] = o_t.T.astype(o_ref.dtype)


def _mla_call(qm, km, vmt, *, t, hp, wide):
    B, S, _ = qm.shape
    return pl.pallas_call(
        functools.partial(_mla_kernel, t=t, hp=hp, wide=wide),
        grid=(B, MLA_HEADS // hp, S // t),
        in_specs=[pl.BlockSpec((None, t, hp * LANES), lambda b, g, qi: (b, qi, g)),
                  pl.BlockSpec((None, S, hp * LANES), lambda b, g, qi: (b, 0, g)),
                  pl.BlockSpec((hp * MLA_V, S), lambda b, g, qi: (g, b))],
        out_specs=pl.BlockSpec((None, t, hp * MLA_V), lambda b, g, qi: (b, qi, g)),
        out_shape=jax.ShapeDtypeStruct((B, S, MLA_HEADS * MLA_V), BF16),
        scratch_shapes=[pltpu.VMEM((hp, MLA_V + ONES_ROWS, t), F32),
                        pltpu.VMEM((hp, 1, t), F32),
                        pltpu.VMEM((hp, LANES, t), BF16),
                        pltpu.VMEM((2, hp, wide * t, t), BF16),
                        pltpu.VMEM((2, hp, 1, t), F32),
                        pltpu.VMEM((hp, 1, t), F32)],
        compiler_params=pltpu.CompilerParams(dimension_semantics=("parallel", "parallel", "arbitrary"),
                                             vmem_limit_bytes=VMEM_LIMIT),
        name="mla_attn",
    )(qm, km, vmt)


def _sb_kernel(q_ref, k_ref, vt_ref, tri_ref, o_ref, acc_ref, carry_ref, qt_ref, a_ref, tot_ref,
               *, t, hp, wide, tiles):
    step_i = pl.program_id(2)
    acc_ref[...] = jnp.zeros_like(acc_ref)
    carry_ref[...] = jnp.zeros_like(carry_ref)
    q_lane = lax.broadcasted_iota(jnp.int32, (t, LANES), 1)
    for tile in range(tiles):
        for pair in range(hp // HEAD_PAIR):
            q = q_ref[tile * t:(tile + 1) * t, pair * LANES:(pair + 1) * LANES].astype(F32)
            qt_ref[tile, pair * HEAD_PAIR] = jnp.where(q_lane < SB_DIM, q, 0.0).T.astype(BF16)
            qt_ref[tile, pair * HEAD_PAIR + 1] = jnp.where(q_lane >= SB_DIM, q, 0.0).T.astype(BF16)
    key_l = lax.broadcasted_iota(jnp.int32, (t, t), 0)
    qry_l = lax.broadcasted_iota(jnp.int32, (t, t), 1)

    valid = key_l < qry_l

    def tile_qi(tile):
        return step_i * tiles + tile

    def key_offset(j):
        return pl.multiple_of(j * t, t)

    def scores(items):
        zs = {}
        for n, (tile, j) in enumerate(items):
            for h in range(hp):
                pair = h // HEAD_PAIR
                k_blk = k_ref[pl.ds(key_offset(j), t), pair * LANES:(pair + 1) * LANES]
                zs[n, h] = _dot(k_blk, qt_ref[tile, h])
        return zs

    def suffix(z, masked):
        sp = jnp.maximum(z, jnp.log2(1.0 + jnp.exp2(jnp.minimum(z, SOFTPLUS_CLAMP))))
        if masked:
            sp = jnp.where(valid, sp, 0.0)
        c = _dot(tri_ref[...], sp.astype(BF16))
        a = jnp.exp2(z - c)
        if masked:
            a = jnp.where(valid, a, 0.0)
        return a.astype(BF16), c[0:1, :]

    def weights_pv(tile, h, j, a, total):
        vt = vt_ref[h * SB_DIM:(h + 1) * SB_DIM, pl.ds(key_offset(j), t)]
        carry = carry_ref[tile, h]
        acc_ref[tile, h] += _dot(vt, a) * jnp.exp2(carry)
        carry_ref[tile, h] = carry - total

    def step(items, masked):
        zs = scores(items)
        mid = {nh: suffix(z, masked) for nh, z in zs.items()}
        for n, (tile, j) in enumerate(items):
            for h in range(hp):
                weights_pv(tile, h, j, *mid[n, h])

    step([(tile, tile_qi(tile)) for tile in range(tiles)], True)

    @pl.when(step_i >= 1)
    def _():
        step([(tile, tile_qi(tile) - 1) for tile in range(tiles)], False)

    if tiles > 1:
        @pl.when(step_i == 0)
        def _():
            step([(tile, tile_qi(tile) - 1) for tile in range(1, tiles)], False)

    def rest_of_tile(tile, c):
        qi = tile_qi(tile)

        def group_items(g):
            return [(tile, qi - 2 - g * wide - b) for b in range(wide)]

        def refill(g, zs):
            for (b, h), z in zs.items():
                a_ref[g % 2, b, h], tot_ref[g % 2, b, h] = suffix(z, False)

        def back(g):
            for b, (_, j) in enumerate(group_items(g)):
                for h in range(hp):
                    weights_pv(tile, h, j, a_ref[g % 2, b, h], tot_ref[g % 2, b, h])

        def alive():
            return (jnp.max(carry_ref[tile]) > EXP2_IS_ZERO_BELOW).astype(jnp.int32)

        n_rest = jnp.maximum(qi - 1, 0)
        n_wide = n_rest // wide
        live_0 = alive()

        @pl.when(jnp.logical_and(n_wide >= 1, live_0 == 1))
        def _():
            refill(0, scores(group_items(0)))

        def cond(state):
            g, live = state
            return jnp.logical_and(g < n_wide, live == 1)

        def body(state):
            g, _ = state
            zs = scores(group_items(g))
            back(g - 1)
            refill(g, zs)
            return g + 1, alive()

        g_end, live = lax.while_loop(cond, body, (jnp.int32(1), live_0))

        @pl.when(jnp.logical_and(n_wide >= 1, live == 1))
        def _():
            back(g_end - 1)

        for r in range(wide - 1):
            @pl.when(jnp.logical_and(n_rest - n_wide * wide > r, live == 1))
            def _():
                step([(tile, qi - 2 - n_wide * wide - r)], False)
        return c

    @pl.when(jnp.max(carry_ref[...]) > EXP2_IS_ZERO_BELOW)
    def _():
        lax.fori_loop(0, tiles, rest_of_tile, 0)

    for tile in range(tiles):
        for pair in range(hp // HEAD_PAIR):
            o_t = jnp.concatenate([acc_ref[tile, pair * HEAD_PAIR], acc_ref[tile, pair * HEAD_PAIR + 1]], axis=0)
            o_ref[tile * t:(tile + 1) * t, pair * LANES:(pair + 1) * LANES] = o_t.T.astype(o_ref.dtype)


def _sb_call(qs, ks, vst, *, t, hp, wide, tiles):
    B, S, _ = qs.shape
    r = lax.broadcasted_iota(jnp.int32, (t, t), 0)
    c = lax.broadcasted_iota(jnp.int32, (t, t), 1)
    tri = (c >= r).astype(BF16)
    w = hp * SB_DIM
    return pl.pallas_call(
        functools.partial(_sb_kernel, t=t, hp=hp, wide=wide, tiles=tiles),
        grid=(B, SB_HEADS // hp, S // (tiles * t)),
        in_specs=[pl.BlockSpec((None, tiles * t, w), lambda b, g, i: (b, i, g)),
                  pl.BlockSpec((None, S, w), lambda b, g, i: (b, 0, g)),
                  pl.BlockSpec((w, S), lambda b, g, i: (g, b)),
                  pl.BlockSpec((t, t), lambda b, g, i: (0, 0))],
        out_specs=pl.BlockSpec((None, tiles * t, w), lambda b, g, i: (b, i, g)),
        out_shape=jax.ShapeDtypeStruct((B, S, SB_HEADS * SB_DIM), BF16),
        scratch_shapes=[pltpu.VMEM((tiles, hp, SB_DIM, t), F32),
                        pltpu.VMEM((tiles, hp, 1, t), F32),
                        pltpu.VMEM((tiles, hp, LANES, t), BF16),
                        pltpu.VMEM((2, wide, hp, t, t), BF16),
                        pltpu.VMEM((2, wide, hp, 1, t), F32)],
        compiler_params=pltpu.CompilerParams(dimension_semantics=("parallel", "parallel", "arbitrary"),
                                             vmem_limit_bytes=VMEM_LIMIT),
        name="sb_attn",
    )(qs, ks, vst, tri)


def _post_kernel(x_ref, oa_ref, ob_ref, p_ref, g_mix_ref, wga_ref, wgb_ref, wbra_ref, wbrb_ref, wout_ref,
                 g_ffn_ref, wfg_ref, wfu_ref, wfd_ref, wpg_ref, wpp_ref, g_ple_ref, g_fin_ref, out_ref,
                 *, ff_chunk, final_norm):
    x = x_ref[...]
    n = _rms(x, g_mix_ref[...]).astype(BF16)
    merged = (jax.nn.sigmoid(_dot(n, wga_ref[...])) * _dot(oa_ref[...], wbra_ref[...])
              + jax.nn.sigmoid(_dot(n, wgb_ref[...])) * _dot(ob_ref[...], wbrb_ref[...]))
    h = x + _dot(merged.astype(BF16), wout_ref[...])

    n2 = _rms(h, g_ffn_ref[...]).astype(BF16)
    d_ff = wfg_ref.shape[1]
    ff = None
    for c0 in range(0, d_ff, ff_chunk):
        c1 = min(c0 + ff_chunk, d_ff)
        g = _dot(n2, wfg_ref[:, c0:c1])
        u = _dot(n2, wfu_ref[:, c0:c1])
        part = _dot((g * jax.nn.sigmoid(g) * u).astype(BF16), wfd_ref[c0:c1, :])
        ff = part if ff is None else ff + part
    h = h + ff

    e = _rms(_dot(p_ref[...].astype(BF16), wpp_ref[...]), g_ple_ref[...])
    h = h + jax.nn.sigmoid(_dot(h.astype(BF16), wpg_ref[...])) * e
    out_ref[...] = _rms(h, g_fin_ref[...]) if final_norm else h


def _post_call(x2, oa, ob, p2, consts, *, tm, ff_chunk, final_norm):
    T, D = x2.shape
    row = lambda w: pl.BlockSpec((tm, w), lambda i: (i, 0))
    return pl.pallas_call(
        functools.partial(_post_kernel, ff_chunk=ff_chunk, final_norm=final_norm),
        grid=(T // tm,),
        in_specs=[row(D), row(oa.shape[1]), row(ob.shape[1]), row(p2.shape[1])]
                 + [_const_spec(c.shape) for c in consts],
        out_specs=row(D),
        out_shape=jax.ShapeDtypeStruct((T, D), F32),
        compiler_params=pltpu.CompilerParams(dimension_semantics=("parallel",),
                                             vmem_limit_bytes=VMEM_LIMIT),
        name="post",
    )(x2, oa, ob, p2, *consts)


def _rotate_half_cols(w):
    half = w.shape[-1] // 2
    return jnp.concatenate([-w[..., half:], w[..., :half]], axis=-1)


def _layer_weights(w_in, w_q_b, w_kv_b):
    d = w_in.shape[0]
    o = 0
    cols = []
    for wd in (MLA_Q_RANK, MLA_KV_RANK, MLA_ROPE, SB_HEADS * SB_DIM, SB_HEADS * SB_DIM, SB_HEADS * SB_DIM, d, d):
        cols.append(w_in[:, o:o + wd])
        o += wd
    w_cq, w_ckv, w_kpe, w_qs, w_ks, w_vs, w_ga, w_gb = cols

    kpe_tile = jnp.concatenate([jnp.zeros_like(w_in[:, :MLA_NOPE]), w_kpe, _rotate_half_cols(w_kpe)], axis=1)
    wa = jnp.concatenate([w_cq, w_ckv, kpe_tile], axis=1)
    wsb = jnp.concatenate([w_qs, w_ks, w_vs], axis=1)

    qb = w_q_b.reshape(MLA_Q_RANK, MLA_HEADS, MLA_NOPE + MLA_ROPE)
    wq = jnp.concatenate([qb, _rotate_half_cols(qb[..., MLA_NOPE:])], axis=-1).reshape(MLA_Q_RANK, MLA_HEADS * LANES)

    kvb = w_kv_b.reshape(MLA_KV_RANK, MLA_HEADS, MLA_NOPE + MLA_V)
    wk = jnp.pad(kvb[..., :MLA_NOPE], ((0, 0), (0, 0), (0, LANES - MLA_NOPE))).reshape(MLA_KV_RANK, MLA_HEADS * LANES)
    wv = kvb[..., MLA_NOPE:].reshape(MLA_KV_RANK, MLA_HEADS * MLA_V)
    bf = lambda t: t.astype(BF16)
    return tuple(map(bf, (wa, wsb, wq, wk, wv, w_ga, w_gb)))


def _rope_freq_row():
    inv_freq = 1.0 / (ROPE_THETA ** (jnp.arange(0, MLA_ROPE, 2, dtype=F32) / MLA_ROPE))
    both = jnp.concatenate([inv_freq, inv_freq])
    return jnp.pad(both, (MLA_NOPE, LANES - MLA_NOPE - MLA_ROPE)).reshape(1, LANES)


def _tiles(B, S):
    T = B * S
    tm_proj = min(1024, T)
    tm_post = min(512, T)
    t_sb = min(256, S)
    t_mla = min(512, S)
    heads_per_step = 8
    sb_wide = 2
    mla_wide = 1
    sb_tiles = 2
    return tm_proj, tm_post, t_sb, t_mla, heads_per_step, sb_wide, mla_wide, sb_tiles


def kernel(x, p, positions, g_mix, w_in, g_q_a, w_q_b, g_kv_a, w_kv_b, w_br_mla, w_br_sb, w_out, g_ffn,
           w_ffn_gate, w_ffn_up, w_ffn_down, w_ple_gate, w_ple_proj, g_ple, g_final):
    B, S, D = x.shape
    T = B * S
    depth = w_in.shape[0]
    tm_proj, tm_post, t_sb, t_mla, hp, sb_wide, mla_wide, sb_tiles = _tiles(B, S)
    d_ff = w_ffn_gate.shape[-1]
    ff_chunk = min(d_ff, 1024)
    row = lambda g: g.reshape(1, -1).astype(F32)
    bf = lambda t: t.astype(BF16)

    pos2 = positions.reshape(T, 1).astype(jnp.int32)
    freq = _rope_freq_row()
    h = x.reshape(T, D)
    for i in range(depth):
        wa, wsb, wq, wk, wv, w_ga, w_gb = _layer_weights(w_in[i], w_q_b[i], w_kv_b[i])
        qm, km, vm, qs, ks, vs = _proj_call(h, pos2, freq, row(g_mix[i]), wa, wsb, row(g_q_a[i]), wq,
                                            row(g_kv_a[i]), wk, wv, tm=tm_proj)
        shp = lambda t: t.reshape(B, S, t.shape[-1])
        o_a = _mla_call(shp(qm), shp(km), vm, t=t_mla, hp=hp, wide=mla_wide).reshape(T, -1)
        o_b = _sb_call(shp(qs), shp(ks), vs, t=t_sb, hp=hp, wide=sb_wide,
                       tiles=sb_tiles if (S // t_sb) % sb_tiles == 0 else 1).reshape(T, -1)
        consts = (row(g_mix[i]), w_ga, w_gb, bf(w_br_mla[i]), bf(w_br_sb[i]), bf(w_out[i]), row(g_ffn[i]),
                  bf(w_ffn_gate[i]), bf(w_ffn_up[i]), bf(w_ffn_down[i]), bf(w_ple_gate[i]), bf(w_ple_proj[i]),
                  row(g_ple[i]), row(g_final))
        h = _post_call(h, o_a, o_b, p[i].reshape(T, -1), consts, tm=tm_post, ff_chunk=ff_chunk,
                       final_norm=(i == depth - 1))
    return h.reshape(B, S, D)
```

```python
import functools
import math

import jax
import jax.numpy as jnp
from jax import lax
from jax.experimental import pallas as pl
from jax.experimental.pallas import tpu as pltpu

EPS = 1e-6
MLA_HEADS = 8
MLA_NOPE = 64
MLA_ROPE = 32
MLA_V = 64
MLA_Q_RANK = 384
MLA_KV_RANK = 256
ROPE_THETA = 10000.0
MLA_SCALE = 1.0 / math.sqrt(MLA_NOPE + MLA_ROPE)
SB_HEADS = 8
SB_DIM = 64
SB_SCALE = 1.0 / math.sqrt(SB_DIM)
NEG_INF = -1e30
LOG2E = math.log2(math.e)
SOFTPLUS_CLAMP = 64.0
EXP2_IS_ZERO_BELOW = -151.0

LANES = 128
HEAD_PAIR = 2
VMEM_LIMIT = 56 * 1024 * 1024

F32 = jnp.float32
BF16 = jnp.bfloat16


def _rms(x, g):
    return x * lax.rsqrt(jnp.mean(x * x, axis=-1, keepdims=True) + EPS) * g


def _dot(a, b):
    return jnp.dot(a, b, preferred_element_type=F32)


def _proj_kernel(x_ref, pos_ref, freq_ref, g_mix_ref, wa_ref, wsb_ref, gq_ref, wq_ref,
                 gkv_ref, wk_ref, wv_ref,
                 qm_ref, km_ref, vm_ref, qs_ref, ks_ref, vs_ref):
    n = _rms(x_ref[...], g_mix_ref[...]).astype(BF16)
    pa = _dot(n, wa_ref[...])
    sb = _dot(n, wsb_ref[...])
    c_q = pa[:, :MLA_Q_RANK]
    c_kv = pa[:, MLA_Q_RANK:MLA_Q_RANK + MLA_KV_RANK]
    kpe = pa[:, MLA_Q_RANK + MLA_KV_RANK:]

    ang = pos_ref[...].astype(F32) * freq_ref[...]
    lane = lax.broadcasted_iota(jnp.int32, ang.shape, 1)
    cos_t = jnp.where(lane < MLA_NOPE + MLA_ROPE, jnp.cos(ang), 0.0)
    sin_t = jnp.sin(ang)

    def rope(tile):
        return tile * cos_t + pltpu.roll(tile, LANES - MLA_ROPE, 1) * sin_t

    nq = _rms(c_q, gq_ref[...]).astype(BF16)
    q = _dot(nq, wq_ref[...])
    nkv = _rms(c_kv, gkv_ref[...]).astype(BF16)
    kn = _dot(nkv, wk_ref[...])
    k_pe = rope(kpe)
    for h in range(MLA_HEADS):
        sl = slice(h * LANES, (h + 1) * LANES)
        qm_ref[sl, :] = (rope(q[:, sl]) * (MLA_SCALE * LOG2E)).astype(BF16).T
        km_ref[:, sl] = (kn[:, sl] + k_pe).astype(BF16)
    vm_ref[...] = _dot(nkv, wv_ref[...]).astype(BF16).T

    w = SB_HEADS * SB_DIM
    qs_ref[...] = (sb[:, :w] * (SB_SCALE * LOG2E)).astype(BF16)
    ks_ref[...] = sb[:, w:2 * w].astype(BF16)
    vs_ref[...] = sb[:, 2 * w:].astype(BF16).T


def _const_spec(shape):
    return pl.BlockSpec(shape, lambda *_: (0,) * len(shape), pipeline_mode=pl.Buffered(1))


def _proj_call(x2, pos2, freq, g_mix, wa, wsb, gq, wq, gkv, wk, wv, *, tm):
    T, D = x2.shape
    row = lambda w: pl.BlockSpec((tm, w), lambda i: (i, 0))
    consts = (freq, g_mix, wa, wsb, gq, wq, gkv, wk, wv)
    col = lambda w: pl.BlockSpec((w, tm), lambda i: (0, i))
    w_mla, w_v, w_sb = MLA_HEADS * LANES, MLA_HEADS * MLA_V, SB_HEADS * SB_DIM
    tok = lambda w: jax.ShapeDtypeStruct((T, w), BF16)
    tok_t = lambda w: jax.ShapeDtypeStruct((w, T), BF16)
    return pl.pallas_call(
        _proj_kernel,
        grid=(T // tm,),
        in_specs=[row(D), row(1)] + [_const_spec(c.shape) for c in consts],
        out_specs=[col(w_mla), row(w_mla), col(w_v), row(w_sb), row(w_sb), col(w_sb)],
        out_shape=[tok_t(w_mla), tok(w_mla), tok_t(w_v), tok(w_sb), tok(w_sb), tok_t(w_sb)],
        compiler_params=pltpu.CompilerParams(dimension_semantics=("parallel",),
                                             vmem_limit_bytes=VMEM_LIMIT),
        name="proj",
    )(x2, pos2, *consts)


ONES_ROWS = 16
MAX_LAG = 64.0


def _mla_kernel(qt_ref, k_ref, vt_ref, o_ref, acc_ref, m_ref, p_ref, scale_ref, lag_ref, *, t, hp, wide):
    qi = pl.program_id(2)
    key_l = lax.broadcasted_iota(jnp.int32, (t, t), 0)
    qry_l = lax.broadcasted_iota(jnp.int32, (t, t), 1)

    def scores(j, n_blk):
        k0 = pl.multiple_of(j * t, t)
        return [_dot(k_ref[pl.ds(k0, n_blk * t), h * LANES:(h + 1) * LANES], qt_ref[h * LANES:(h + 1) * LANES, :])
                for h in range(hp)]

    def values(h, j, n_blk):
        k0 = pl.multiple_of(j * t, t)
        w = n_blk * t
        return jnp.concatenate([vt_ref[h * MLA_V:(h + 1) * MLA_V, pl.ds(k0, w)],
                                jnp.ones((ONES_ROWS, w), BF16)], axis=0)

    def one_pass(h, s, diag_last, first):
        if diag_last:
            n_keys = s.shape[0]
            key_g = lax.broadcasted_iota(jnp.int32, (n_keys, t), 0)
            qry_g = lax.broadcasted_iota(jnp.int32, (n_keys, t), 1) + (n_keys - t)
            s = jnp.where(key_g <= qry_g, s, NEG_INF)
        m_old = s[0:1, :] if first else m_ref[h]
        p = jnp.exp2(s - m_old).astype(BF16)
        lag = jnp.maximum(jnp.max(s, axis=0, keepdims=True) - m_old, 0.0)
        m_ref[h] = m_old + lag
        lag_ref[h] = jnp.maximum(lag_ref[h], lag)
        return p, jnp.exp2(-lag)

    def add_pv(h, j, n_blk, p, scale):
        acc_ref[h] = (acc_ref[h] + _dot(values(h, j, n_blk), p)) * scale

    def refill(g, slot, s_all, first):
        for h in range(hp):
            p_ref[slot, h], scale_ref[slot, h] = one_pass(h, s_all[h], False, first)

    def back(g, slot):
        for h in range(hp):
            add_pv(h, g * wide, wide, p_ref[slot, h], scale_ref[slot, h])

    def pipelined(g, slot):
        s_all = scores(g * wide, wide)
        back(g - 1, 1 - slot)
        refill(g, slot, s_all, False)

    def body(i, c):
        pipelined(2 * i + 1, 1)
        pipelined(2 * i + 2, 0)
        return c

    def tail(n_blk, pending_slot):
        j = qi + 1 - n_blk
        s_all = scores(j, n_blk)
        if pending_slot is not None:
            back(n_groups - 1, pending_slot)
        for h in range(hp):
            add_pv(h, j, n_blk, *one_pass(h, s_all[h], True, pending_slot is None))

    acc_ref[...] = jnp.zeros_like(acc_ref)
    lag_ref[...] = jnp.zeros_like(lag_ref)
    n_groups = qi // wide
    n_pairs = jnp.maximum(n_groups - 1, 0) // 2

    @pl.when(n_groups >= 1)
    def _():
        refill(0, 0, scores(0, wide), True)

    lax.fori_loop(0, n_pairs, body, 0)

    @pl.when(jnp.logical_and(n_groups >= 2, n_groups % 2 == 0))
    def _():
        pipelined(n_groups - 1, 1)

    for n_blk in range(1, wide + 1):
        is_size = qi % wide == n_blk - 1
        pl.when(jnp.logical_and(is_size, n_groups == 0))(functools.partial(tail, n_blk, None))
        for slot in (0, 1):
            has_slot = jnp.logical_and(n_groups >= 1, (n_groups - 1) % 2 == slot)
            pl.when(jnp.logical_and(is_size, has_slot))(functools.partial(tail, n_blk, slot))

    def two_pass_step(j, masked):
        s_all = scores(j, 1)
        for h in range(hp):
            s = s_all[h]
            if masked:
                s = jnp.where(key_l <= qry_l, s, NEG_INF)
            m_prev = m_ref[h]
            m_new = jnp.maximum(m_prev, jnp.max(s, axis=0, keepdims=True))
            m_ref[h] = m_new
            acc_ref[h] = jnp.exp2(m_prev - m_new) * acc_ref[h] + _dot(values(h, j, 1), jnp.exp2(s - m_new).astype(BF16))

    @pl.when(jnp.max(lag_ref[...]) > MAX_LAG)
    def _():
        acc_ref[...] = jnp.zeros_like(acc_ref)
        m_ref[...] = jnp.full_like(m_ref, NEG_INF)
        lax.fori_loop(0, qi, lambda j, c: (two_pass_step(j, False), c)[1], 0)
        two_pass_step(qi, True)

    for pair in range(hp // HEAD_PAIR):
        outs = []
        for h in (pair * HEAD_PAIR, pair * HEAD_PAIR + 1):
            a = acc_ref[h]
            outs.append(a[:MLA_V] / a[MLA_V:MLA_V + 1])
        o_t = jnp.concatenate(outs, axis=0)
        o_ref[:, pair * LANES:(pair + 1) * LANES] = o_t.T.astype(o_ref.dtype)


def _mla_call(qmt, km, vmt, *, t, hp, wide):
    B, S, _ = km.shape
    n_q = S // t
    return pl.pallas_call(
        functools.partial(_mla_kernel, t=t, hp=hp, wide=wide),
        grid=(B, MLA_HEADS // hp, n_q),
        in_specs=[pl.BlockSpec((hp * LANES, t), lambda b, g, qi: (g, b * n_q + qi)),
                  pl.BlockSpec((None, S, hp * LANES), lambda b, g, qi: (b, 0, g)),
                  pl.BlockSpec((hp * MLA_V, S), lambda b, g, qi: (g, b))],
        out_specs=pl.BlockSpec((None, t, hp * MLA_V), lambda b, g, qi: (b, qi, g)),
        out_shape=jax.ShapeDtypeStruct((B, S, MLA_HEADS * MLA_V), BF16),
        scratch_shapes=[pltpu.VMEM((hp, MLA_V + ONES_ROWS, t), F32),
                        pltpu.VMEM((hp, 1, t), F32),
                        pltpu.VMEM((2, hp, wide * t, t), BF16),
                        pltpu.VMEM((2, hp, 1, t), F32),
                        pltpu.VMEM((hp, 1, t), F32)],
        compiler_params=pltpu.CompilerParams(dimension_semantics=("parallel", "parallel", "arbitrary"),
                                             vmem_limit_bytes=VMEM_LIMIT),
        name="mla_attn",
    )(qmt, km, vmt)


def _sb_kernel(q_ref, k_ref, vt_ref, tri_ref, o_ref, acc_ref, carry_ref, qt_ref, a_ref, tot_ref,
               *, t, hp, wide, tiles):
    step_i = pl.program_id(2)
    acc_ref[...] = jnp.zeros_like(acc_ref)
    carry_ref[...] = jnp.zeros_like(carry_ref)
    q_lane = lax.broadcasted_iota(jnp.int32, (t, LANES), 1)
    for tile in range(tiles):
        for pair in range(hp // HEAD_PAIR):
            q = q_ref[tile * t:(tile + 1) * t, pair * LANES:(pair + 1) * LANES].astype(F32)
            qt_ref[tile, pair * HEAD_PAIR] = jnp.where(q_lane < SB_DIM, q, 0.0).T.astype(BF16)
            qt_ref[tile, pair * HEAD_PAIR + 1] = jnp.where(q_lane >= SB_DIM, q, 0.0).T.astype(BF16)
    key_l = lax.broadcasted_iota(jnp.int32, (t, t), 0)
    qry_l = lax.broadcasted_iota(jnp.int32, (t, t), 1)

    valid = key_l < qry_l

    def tile_qi(tile):
        return step_i * tiles + tile

    def key_offset(j):
        return pl.multiple_of(j * t, t)

    def scores(items):
        zs = {}
        for n, (tile, j) in enumerate(items):
            for h in range(hp):
                pair = h // HEAD_PAIR
                k_blk = k_ref[pl.ds(key_offset(j), t), pair * LANES:(pair + 1) * LANES]
                zs[n, h] = _dot(k_blk, qt_ref[tile, h])
        return zs

    def suffix(z, masked):
        sp = jnp.maximum(z, jnp.log2(1.0 + jnp.exp2(jnp.minimum(z, SOFTPLUS_CLAMP))))
        if masked:
            sp = jnp.where(valid, sp, 0.0)
        c = _dot(tri_ref[...], sp.astype(BF16))
        a = jnp.exp2(z - c)
        if masked:
            a = jnp.where(valid, a, 0.0)
        return a.astype(BF16), c[0:1, :]

    def weights_pv(tile, h, j, a, total):
        vt = vt_ref[h * SB_DIM:(h + 1) * SB_DIM, pl.ds(key_offset(j), t)]
        carry = carry_ref[tile, h]
        acc_ref[tile, h] += _dot(vt, a) * jnp.exp2(carry)
        carry_ref[tile, h] = carry - total

    def step(items, masked):
        zs = scores(items)
        mid = {nh: suffix(z, masked) for nh, z in zs.items()}
        for n, (tile, j) in enumerate(items):
            for h in range(hp):
                weights_pv(tile, h, j, *mid[n, h])

    step([(tile, tile_qi(tile)) for tile in range(tiles)], True)

    @pl.when(step_i >= 1)
    def _():
        step([(tile, tile_qi(tile) - 1) for tile in range(tiles)], False)

    if tiles > 1:
        @pl.when(step_i == 0)
        def _():
            step([(tile, tile_qi(tile) - 1) for tile in range(1, tiles)], False)

    def rest_of_tile(tile, c):
        qi = tile_qi(tile)

        def group_items(g):
            return [(tile, qi - 2 - g * wide - b) for b in range(wide)]

        def refill(g, zs):
            for (b, h), z in zs.items():
                a_ref[g % 2, b, h], tot_ref[g % 2, b, h] = suffix(z, False)

        def back(g):
            for b, (_, j) in enumerate(group_items(g)):
                for h in range(hp):
                    weights_pv(tile, h, j, a_ref[g % 2, b, h], tot_ref[g % 2, b, h])

        def alive():
            return (jnp.max(carry_ref[tile]) > EXP2_IS_ZERO_BELOW).astype(jnp.int32)

        n_rest = jnp.maximum(qi - 1, 0)
        n_wide = n_rest // wide
        live_0 = alive()

        @pl.when(jnp.logical_and(n_wide >= 1, live_0 == 1))
        def _():
            refill(0, scores(group_items(0)))

        def cond(state):
            g, live = state
            return jnp.logical_and(g < n_wide, live == 1)

        def body(state):
            g, _ = state
            zs = scores(group_items(g))
            back(g - 1)
            refill(g, zs)
            return g + 1, alive()

        g_end, live = lax.while_loop(cond, body, (jnp.int32(1), live_0))

        @pl.when(jnp.logical_and(n_wide >= 1, live == 1))
        def _():
            back(g_end - 1)

        for r in range(wide - 1):
            @pl.when(jnp.logical_and(n_rest - n_wide * wide > r, live == 1))
            def _():
                step([(tile, qi - 2 - n_wide * wide - r)], False)
        return c

    @pl.when(jnp.max(carry_ref[...]) > EXP2_IS_ZERO_BELOW)
    def _():
        lax.fori_loop(0, tiles, rest_of_tile, 0)

    for tile in range(tiles):
        for pair in range(hp // HEAD_PAIR):
            o_t = jnp.concatenate([acc_ref[tile, pair * HEAD_PAIR], acc_ref[tile, pair * HEAD_PAIR + 1]], axis=0)
            o_ref[tile * t:(tile + 1) * t, pair * LANES:(pair + 1) * LANES] = o_t.T.astype(o_ref.dtype)


def _sb_call(qs, ks, vst, *, t, hp, wide, tiles):
    B, S, _ = qs.shape
    r = lax.broadcasted_iota(jnp.int32, (t, t), 0)
    c = lax.broadcasted_iota(jnp.int32, (t, t), 1)
    tri = (c >= r).astype(BF16)
    w = hp * SB_DIM
    return pl.pallas_call(
        functools.partial(_sb_kernel, t=t, hp=hp, wide=wide, tiles=tiles),
        grid=(B, SB_HEADS // hp, S // (tiles * t)),
        in_specs=[pl.BlockSpec((None, tiles * t, w), lambda b, g, i: (b, i, g)),
                  pl.BlockSpec((None, S, w), lambda b, g, i: (b, 0, g)),
                  pl.BlockSpec((w, S), lambda b, g, i: (g, b)),
                  pl.BlockSpec((t, t), lambda b, g, i: (0, 0))],
        out_specs=pl.BlockSpec((None, tiles * t, w), lambda b, g, i: (b, i, g)),
        out_shape=jax.ShapeDtypeStruct((B, S, SB_HEADS * SB_DIM), BF16),
        scratch_shapes=[pltpu.VMEM((tiles, hp, SB_DIM, t), F32),
                        pltpu.VMEM((tiles, hp, 1, t), F32),
                        pltpu.VMEM((tiles, hp, LANES, t), BF16),
                        pltpu.VMEM((2, wide, hp, t, t), BF16),
                        pltpu.VMEM((2, wide, hp, 1, t), F32)],
        compiler_params=pltpu.CompilerParams(dimension_semantics=("parallel", "parallel", "arbitrary"),
                                             vmem_limit_bytes=VMEM_LIMIT),
        name="sb_attn",
    )(qs, ks, vst, tri)


def _post_kernel(x_ref, oa_ref, ob_ref, p_ref, g_mix_ref, wga_ref, wgb_ref, wbra_ref, wbrb_ref, wout_ref,
                 g_ffn_ref, wfg_ref, wfu_ref, wfd_ref, wpg_ref, wpp_ref, g_ple_ref, g_fin_ref, out_ref,
                 *, ff_chunk, final_norm):
    x = x_ref[...]
    n = _rms(x, g_mix_ref[...]).astype(BF16)
    merged = (jax.nn.sigmoid(_dot(n, wga_ref[...])) * _dot(oa_ref[...], wbra_ref[...])
              + jax.nn.sigmoid(_dot(n, wgb_ref[...])) * _dot(ob_ref[...], wbrb_ref[...]))
    h = x + _dot(merged.astype(BF16), wout_ref[...])

    n2 = _rms(h, g_ffn_ref[...]).astype(BF16)
    d_ff = wfg_ref.shape[1]
    ff = None
    for c0 in range(0, d_ff, ff_chunk):
        c1 = min(c0 + ff_chunk, d_ff)
        g = _dot(n2, wfg_ref[:, c0:c1])
        u = _dot(n2, wfu_ref[:, c0:c1])
        part = _dot((g * jax.nn.sigmoid(g) * u).astype(BF16), wfd_ref[c0:c1, :])
        ff = part if ff is None else ff + part
    h = h + ff

    e = _rms(_dot(p_ref[...].astype(BF16), wpp_ref[...]), g_ple_ref[...])
    h = h + jax.nn.sigmoid(_dot(h.astype(BF16), wpg_ref[...])) * e
    out_ref[...] = _rms(h, g_fin_ref[...]) if final_norm else h


def _post_call(x2, oa, ob, p2, consts, *, tm, ff_chunk, final_norm):
    T, D = x2.shape
    row = lambda w: pl.BlockSpec((tm, w), lambda i: (i, 0))
    return pl.pallas_call(
        functools.partial(_post_kernel, ff_chunk=ff_chunk, final_norm=final_norm),
        grid=(T // tm,),
        in_specs=[row(D), row(oa.shape[1]), row(ob.shape[1]), row(p2.shape[1])]
                 + [_const_spec(c.shape) for c in consts],
        out_specs=row(D),
        out_shape=jax.ShapeDtypeStruct((T, D), F32),
        compiler_params=pltpu.CompilerParams(dimension_semantics=("parallel",),
                                             vmem_limit_bytes=VMEM_LIMIT),
        name="post",
    )(x2, oa, ob, p2, *consts)


def _rotate_half_cols(w):
    half = w.shape[-1] // 2
    return jnp.concatenate([-w[..., half:], w[..., :half]], axis=-1)


def _layer_weights(w_in, w_q_b, w_kv_b):
    d = w_in.shape[0]
    o = 0
    cols = []
    for wd in (MLA_Q_RANK, MLA_KV_RANK, MLA_ROPE, SB_HEADS * SB_DIM, SB_HEADS * SB_DIM, SB_HEADS * SB_DIM, d, d):
        cols.append(w_in[:, o:o + wd])
        o += wd
    w_cq, w_ckv, w_kpe, w_qs, w_ks, w_vs, w_ga, w_gb = cols

    kpe_tile = jnp.concatenate([jnp.zeros_like(w_in[:, :MLA_NOPE]), w_kpe, _rotate_half_cols(w_kpe)], axis=1)
    wa = jnp.concatenate([w_cq, w_ckv, kpe_tile], axis=1)
    wsb = jnp.concatenate([w_qs, w_ks, w_vs], axis=1)

    qb = w_q_b.reshape(MLA_Q_RANK, MLA_HEADS, MLA_NOPE + MLA_ROPE)
    wq = jnp.concatenate([qb, _rotate_half_cols(qb[..., MLA_NOPE:])], axis=-1).reshape(MLA_Q_RANK, MLA_HEADS * LANES)

    kvb = w_kv_b.reshape(MLA_KV_RANK, MLA_HEADS, MLA_NOPE + MLA_V)
    wk = jnp.pad(kvb[..., :MLA_NOPE], ((0, 0), (0, 0), (0, LANES - MLA_NOPE))).reshape(MLA_KV_RANK, MLA_HEADS * LANES)
    wv = kvb[..., MLA_NOPE:].reshape(MLA_KV_RANK, MLA_HEADS * MLA_V)
    bf = lambda t: t.astype(BF16)
    return tuple(map(bf, (wa, wsb, wq, wk, wv, w_ga, w_gb)))


def _rope_freq_row():
    inv_freq = 1.0 / (ROPE_THETA ** (jnp.arange(0, MLA_ROPE, 2, dtype=F32) / MLA_ROPE))
    both = jnp.concatenate([inv_freq, inv_freq])
    return jnp.pad(both, (MLA_NOPE, LANES - MLA_NOPE - MLA_ROPE)).reshape(1, LANES)


def _tiles(B, S):
    T = B * S
    tm_proj = min(1024, T)
    tm_post = min(512, T)
    t_sb = min(256, S)
    t_mla = min(512, S)
    heads_per_step = 8
    sb_wide = 2
    mla_wide = 1
    sb_tiles = 2
    return tm_proj, tm_post, t_sb, t_mla, heads_per_step, sb_wide, mla_wide, sb_tiles


def kernel(x, p, positions, g_mix, w_in, g_q_a, w_q_b, g_kv_a, w_kv_b, w_br_mla, w_br_sb, w_out, g_ffn,
           w_ffn_gate, w_ffn_up, w_ffn_down, w_ple_gate, w_ple_proj, g_ple, g_final):
    B, S, D = x.shape
    T = B * S
    depth = w_in.shape[0]
    tm_proj, tm_post, t_sb, t_mla, hp, sb_wide, mla_wide, sb_tiles = _tiles(B, S)
    d_ff = w_ffn_gate.shape[-1]
    ff_chunk = min(d_ff, 1024)
    row = lambda g: g.reshape(1, -1).astype(F32)
    bf = lambda t: t.astype(BF16)

    pos2 = positions.reshape(T, 1).astype(jnp.int32)
    freq = _rope_freq_row()
    h = x.reshape(T, D)
    for i in range(depth):
        wa, wsb, wq, wk, wv, w_ga, w_gb = _layer_weights(w_in[i], w_q_b[i], w_kv_b[i])
        qm, km, vm, qs, ks, vs = _proj_call(h, pos2, freq, row(g_mix[i]), wa, wsb, row(g_q_a[i]), wq,
                                            row(g_kv_a[i]), wk, wv, tm=tm_proj)
        shp = lambda t: t.reshape(B, S, t.shape[-1])
        o_a = _mla_call(qm, shp(km), vm, t=t_mla, hp=hp, wide=mla_wide).reshape(T, -1)
        o_b = _sb_call(shp(qs), shp(ks), vs, t=t_sb, hp=hp, wide=sb_wide,
                       tiles=sb_tiles if (S // t_sb) % sb_tiles == 0 else 1).reshape(T, -1)
        consts = (row(g_mix[i]), w_ga, w_gb, bf(w_br_mla[i]), bf(w_br_sb[i]), bf(w_out[i]), row(g_ffn[i]),
                  bf(w_ffn_gate[i]), bf(w_ffn_up[i]), bf(w_ffn_down[i]), bf(w_ple_gate[i]), bf(w_ple_proj[i]),
                  row(g_ple[i]), row(g_final))
        h = _post_call(h, o_a, o_b, p[i].reshape(T, -1), consts, tm=tm_post, ff_chunk=ff_chunk,
                       final_norm=(i == depth - 1))
    return h.reshape(B, S, D)
```

```python
import functools
import math

import jax
import jax.numpy as jnp
from jax import lax
from jax.experimental import pallas as pl
from jax.experimental.pallas import tpu as pltpu

EPS = 1e-6
MLA_HEADS = 8
MLA_NOPE = 64
MLA_ROPE = 32
MLA_V = 64
MLA_Q_RANK = 384
MLA_KV_RANK = 256
ROPE_THETA = 10000.0
MLA_SCALE = 1.0 / math.sqrt(MLA_NOPE + MLA_ROPE)
SB_HEADS = 8
SB_DIM = 64
SB_SCALE = 1.0 / math.sqrt(SB_DIM)
NEG_INF = -1e30
LOG2E = math.log2(math.e)
SOFTPLUS_CLAMP = 64.0
EXP2_IS_ZERO_BELOW = -151.0

LANES = 128
HEAD_PAIR = 2
VMEM_LIMIT = 56 * 1024 * 1024

F32 = jnp.float32
BF16 = jnp.bfloat16


def _rms(x, g):
    return x * lax.rsqrt(jnp.mean(x * x, axis=-1, keepdims=True) + EPS) * g


def _dot(a, b):
    return jnp.dot(a, b, preferred_element_type=F32)


def _proj_kernel(x_ref, pos_ref, freq_ref, g_mix_ref, wa_ref, wsb_ref, gq_ref, wq_ref,
                 gkv_ref, wk_ref, wv_ref,
                 qm_ref, km_ref, vm_ref, qs_ref, ks_ref, vs_ref):
    n = _rms(x_ref[...], g_mix_ref[...]).astype(BF16)
    pa = _dot(n, wa_ref[...])
    sb = _dot(n, wsb_ref[...])
    c_q = pa[:, :MLA_Q_RANK]
    c_kv = pa[:, MLA_Q_RANK:MLA_Q_RANK + MLA_KV_RANK]
    kpe = pa[:, MLA_Q_RANK + MLA_KV_RANK:]

    ang = pos_ref[...].astype(F32) * freq_ref[...]
    lane = lax.broadcasted_iota(jnp.int32, ang.shape, 1)
    cos_t = jnp.where(lane < MLA_NOPE + MLA_ROPE, jnp.cos(ang), 0.0)
    sin_t = jnp.sin(ang)

    def rope(tile):
        return tile * cos_t + pltpu.roll(tile, LANES - MLA_ROPE, 1) * sin_t

    nq = _rms(c_q, gq_ref[...]).astype(BF16)
    q = _dot(nq, wq_ref[...])
    nkv = _rms(c_kv, gkv_ref[...]).astype(BF16)
    kn = _dot(nkv, wk_ref[...])
    k_pe = rope(kpe)
    for h in range(MLA_HEADS):
        sl = slice(h * LANES, (h + 1) * LANES)
        qm_ref[sl, :] = (rope(q[:, sl]) * (MLA_SCALE * LOG2E)).astype(BF16).T
        km_ref[:, sl] = (kn[:, sl] + k_pe).astype(BF16)
    vm_ref[...] = _dot(nkv, wv_ref[...]).astype(BF16).T

    w = SB_HEADS * SB_DIM
    qs_t = (sb[:, :w] * (SB_SCALE * LOG2E)).astype(BF16).T
    zeros = jnp.zeros((SB_DIM, qs_t.shape[1]), BF16)
    for h in range(SB_HEADS):
        own = qs_t[h * SB_DIM:(h + 1) * SB_DIM]
        qs_ref[h * LANES:(h + 1) * LANES, :] = jnp.concatenate([own, zeros] if h % HEAD_PAIR == 0 else [zeros, own], axis=0)
    ks_ref[...] = sb[:, w:2 * w].astype(BF16)
    vs_ref[...] = sb[:, 2 * w:].astype(BF16).T


def _const_spec(shape):
    return pl.BlockSpec(shape, lambda *_: (0,) * len(shape), pipeline_mode=pl.Buffered(1))


def _proj_call(x2, pos2, freq, g_mix, wa, wsb, gq, wq, gkv, wk, wv, *, tm):
    T, D = x2.shape
    row = lambda w: pl.BlockSpec((tm, w), lambda i: (i, 0))
    consts = (freq, g_mix, wa, wsb, gq, wq, gkv, wk, wv)
    col = lambda w: pl.BlockSpec((w, tm), lambda i: (0, i))
    w_mla, w_v, w_sb = MLA_HEADS * LANES, MLA_HEADS * MLA_V, SB_HEADS * SB_DIM
    tok = lambda w: jax.ShapeDtypeStruct((T, w), BF16)
    tok_t = lambda w: jax.ShapeDtypeStruct((w, T), BF16)
    return pl.pallas_call(
        _proj_kernel,
        grid=(T // tm,),
        in_specs=[row(D), row(1)] + [_const_spec(c.shape) for c in consts],
        out_specs=[col(w_mla), row(w_mla), col(w_v), col(SB_HEADS * LANES), row(w_sb), col(w_sb)],
        out_shape=[tok_t(w_mla), tok(w_mla), tok_t(w_v), tok_t(SB_HEADS * LANES), tok(w_sb), tok_t(w_sb)],
        compiler_params=pltpu.CompilerParams(dimension_semantics=("parallel",),
                                             vmem_limit_bytes=VMEM_LIMIT),
        name="proj",
    )(x2, pos2, *consts)


ONES_ROWS = 16
MAX_LAG = 64.0


def _mla_kernel(qt_ref, k_ref, vt_ref, o_ref, acc_ref, m_ref, p_ref, scale_ref, lag_ref, *, t, hp, wide):
    qi = pl.program_id(2)
    key_l = lax.broadcasted_iota(jnp.int32, (t, t), 0)
    qry_l = lax.broadcasted_iota(jnp.int32, (t, t), 1)

    def scores(j, n_blk):
        k0 = pl.multiple_of(j * t, t)
        return [_dot(k_ref[pl.ds(k0, n_blk * t), h * LANES:(h + 1) * LANES], qt_ref[h * LANES:(h + 1) * LANES, :])
                for h in range(hp)]

    def values(h, j, n_blk):
        k0 = pl.multiple_of(j * t, t)
        w = n_blk * t
        return jnp.concatenate([vt_ref[h * MLA_V:(h + 1) * MLA_V, pl.ds(k0, w)],
                                jnp.ones((ONES_ROWS, w), BF16)], axis=0)

    def one_pass(h, s, diag_last, first):
        if diag_last:
            n_keys = s.shape[0]
            key_g = lax.broadcasted_iota(jnp.int32, (n_keys, t), 0)
            qry_g = lax.broadcasted_iota(jnp.int32, (n_keys, t), 1) + (n_keys - t)
            s = jnp.where(key_g <= qry_g, s, NEG_INF)
        m_old = s[0:1, :] if first else m_ref[h]
        p = jnp.exp2(s - m_old).astype(BF16)
        lag = jnp.maximum(jnp.max(s, axis=0, keepdims=True) - m_old, 0.0)
        m_ref[h] = m_old + lag
        lag_ref[h] = jnp.maximum(lag_ref[h], lag)
        return p, jnp.exp2(-lag)

    def add_pv(h, j, n_blk, p, scale):
        acc_ref[h] = (acc_ref[h] + _dot(values(h, j, n_blk), p)) * scale

    def refill(g, slot, s_all, first):
        for h in range(hp):
            p_ref[slot, h], scale_ref[slot, h] = one_pass(h, s_all[h], False, first)

    def back(g, slot):
        for h in range(hp):
            add_pv(h, g * wide, wide, p_ref[slot, h], scale_ref[slot, h])

    def pipelined(g, slot):
        s_all = scores(g * wide, wide)
        back(g - 1, 1 - slot)
        refill(g, slot, s_all, False)

    def body(i, c):
        pipelined(2 * i + 1, 1)
        pipelined(2 * i + 2, 0)
        return c

    def tail(n_blk, pending_slot):
        j = qi + 1 - n_blk
        s_all = scores(j, n_blk)
        if pending_slot is not None:
            back(n_groups - 1, pending_slot)
        for h in range(hp):
            add_pv(h, j, n_blk, *one_pass(h, s_all[h], True, pending_slot is None))

    acc_ref[...] = jnp.zeros_like(acc_ref)
    lag_ref[...] = jnp.zeros_like(lag_ref)
    n_groups = qi // wide
    n_pairs = jnp.maximum(n_groups - 1, 0) // 2

    @pl.when(n_groups >= 1)
    def _():
        refill(0, 0, scores(0, wide), True)

    lax.fori_loop(0, n_pairs, body, 0)

    @pl.when(jnp.logical_and(n_groups >= 2, n_groups % 2 == 0))
    def _():
        pipelined(n_groups - 1, 1)

    for n_blk in range(1, wide + 1):
        is_size = qi % wide == n_blk - 1
        pl.when(jnp.logical_and(is_size, n_groups == 0))(functools.partial(tail, n_blk, None))
        for slot in (0, 1):
            has_slot = jnp.logical_and(n_groups >= 1, (n_groups - 1) % 2 == slot)
            pl.when(jnp.logical_and(is_size, has_slot))(functools.partial(tail, n_blk, slot))

    def two_pass_step(j, masked):
        s_all = scores(j, 1)
        for h in range(hp):
            s = s_all[h]
            if masked:
                s = jnp.where(key_l <= qry_l, s, NEG_INF)
            m_prev = m_ref[h]
            m_new = jnp.maximum(m_prev, jnp.max(s, axis=0, keepdims=True))
            m_ref[h] = m_new
            acc_ref[h] = jnp.exp2(m_prev - m_new) * acc_ref[h] + _dot(values(h, j, 1), jnp.exp2(s - m_new).astype(BF16))

    @pl.when(jnp.max(lag_ref[...]) > MAX_LAG)
    def _():
        acc_ref[...] = jnp.zeros_like(acc_ref)
        m_ref[...] = jnp.full_like(m_ref, NEG_INF)
        lax.fori_loop(0, qi, lambda j, c: (two_pass_step(j, False), c)[1], 0)
        two_pass_step(qi, True)

    for pair in range(hp // HEAD_PAIR):
        outs = []
        for h in (pair * HEAD_PAIR, pair * HEAD_PAIR + 1):
            a = acc_ref[h]
            outs.append(a[:MLA_V] / a[MLA_V:MLA_V + 1])
        o_t = jnp.concatenate(outs, axis=0)
        o_ref[:, pair * LANES:(pair + 1) * LANES] = o_t.T.astype(o_ref.dtype)


def _mla_call(qmt, km, vmt, *, t, hp, wide):
    B, S, _ = km.shape
    n_q = S // t
    return pl.pallas_call(
        functools.partial(_mla_kernel, t=t, hp=hp, wide=wide),
        grid=(B, MLA_HEADS // hp, n_q),
        in_specs=[pl.BlockSpec((hp * LANES, t), lambda b, g, qi: (g, b * n_q + qi)),
                  pl.BlockSpec((None, S, hp * LANES), lambda b, g, qi: (b, 0, g)),
                  pl.BlockSpec((hp * MLA_V, S), lambda b, g, qi: (g, b))],
        out_specs=pl.BlockSpec((None, t, hp * MLA_V), lambda b, g, qi: (b, qi, g)),
        out_shape=jax.ShapeDtypeStruct((B, S, MLA_HEADS * MLA_V), BF16),
        scratch_shapes=[pltpu.VMEM((hp, MLA_V + ONES_ROWS, t), F32),
                        pltpu.VMEM((hp, 1, t), F32),
                        pltpu.VMEM((2, hp, wide * t, t), BF16),
                        pltpu.VMEM((2, hp, 1, t), F32),
                        pltpu.VMEM((hp, 1, t), F32)],
        compiler_params=pltpu.CompilerParams(dimension_semantics=("parallel", "parallel", "arbitrary"),
                                             vmem_limit_bytes=VMEM_LIMIT),
        name="mla_attn",
    )(qmt, km, vmt)


def _sb_kernel(qt_ref, k_ref, vt_ref, tri_ref, o_ref, acc_ref, carry_ref, a_ref, tot_ref,
               *, t, hp, wide, tiles):
    step_i = pl.program_id(2)
    acc_ref[...] = jnp.zeros_like(acc_ref)
    carry_ref[...] = jnp.zeros_like(carry_ref)
    key_l = lax.broadcasted_iota(jnp.int32, (t, t), 0)
    qry_l = lax.broadcasted_iota(jnp.int32, (t, t), 1)

    valid = key_l < qry_l

    def tile_qi(tile):
        return step_i * tiles + tile

    def key_offset(j):
        return pl.multiple_of(j * t, t)

    def scores(items):
        zs = {}
        for n, (tile, j) in enumerate(items):
            for h in range(hp):
                pair = h // HEAD_PAIR
                k_blk = k_ref[pl.ds(key_offset(j), t), pair * LANES:(pair + 1) * LANES]
                cols = pl.ds(tile * t if isinstance(tile, int) else pl.multiple_of(tile * t, t), t)
                q_t = qt_ref[h * LANES:(h + 1) * LANES, cols]
                zs[n, h] = _dot(k_blk, q_t)
        return zs

    def suffix(z, masked):
        sp = jnp.maximum(z, jnp.log2(1.0 + jnp.exp2(jnp.minimum(z, SOFTPLUS_CLAMP))))
        if masked:
            sp = jnp.where(valid, sp, 0.0)
        c = _dot(tri_ref[...], sp.astype(BF16))
        a = jnp.exp2(z - c)
        if masked:
            a = jnp.where(valid, a, 0.0)
        return a.astype(BF16), c[0:1, :]

    def weights_pv(tile, h, j, a, total):
        vt = vt_ref[h * SB_DIM:(h + 1) * SB_DIM, pl.ds(key_offset(j), t)]
        carry = carry_ref[tile, h]
        acc_ref[tile, h] += _dot(vt, a) * jnp.exp2(carry)
        carry_ref[tile, h] = carry - total

    def step(items, masked):
        zs = scores(items)
        mid = {nh: suffix(z, masked) for nh, z in zs.items()}
        for n, (tile, j) in enumerate(items):
            for h in range(hp):
                weights_pv(tile, h, j, *mid[n, h])

    step([(tile, tile_qi(tile)) for tile in range(tiles)], True)

    @pl.when(step_i >= 1)
    def _():
        step([(tile, tile_qi(tile) - 1) for tile in range(tiles)], False)

    if tiles > 1:
        @pl.when(step_i == 0)
        def _():
            step([(tile, tile_qi(tile) - 1) for tile in range(1, tiles)], False)

    def rest_of_tile(tile, c):
        qi = tile_qi(tile)

        def group_items(g):
            return [(tile, qi - 2 - g * wide - b) for b in range(wide)]

        def refill(g, zs):
            for (b, h), z in zs.items():
                a_ref[g % 2, b, h], tot_ref[g % 2, b, h] = suffix(z, False)

        def back(g):
            for b, (_, j) in enumerate(group_items(g)):
                for h in range(hp):
                    weights_pv(tile, h, j, a_ref[g % 2, b, h], tot_ref[g % 2, b, h])

        def alive():
            return (jnp.max(carry_ref[tile]) > EXP2_IS_ZERO_BELOW).astype(jnp.int32)

        n_rest = jnp.maximum(qi - 1, 0)
        n_wide = n_rest // wide
        live_0 = alive()

        @pl.when(jnp.logical_and(n_wide >= 1, live_0 == 1))
        def _():
            refill(0, scores(group_items(0)))

        def cond(state):
            g, live = state
            return jnp.logical_and(g < n_wide, live == 1)

        def body(state):
            g, _ = state
            zs = scores(group_items(g))
            back(g - 1)
            refill(g, zs)
            return g + 1, alive()

        g_end, live = lax.while_loop(cond, body, (jnp.int32(1), live_0))

        @pl.when(jnp.logical_and(n_wide >= 1, live == 1))
        def _():
            back(g_end - 1)

        for r in range(wide - 1):
            @pl.when(jnp.logical_and(n_rest - n_wide * wide > r, live == 1))
            def _():
                step([(tile, qi - 2 - n_wide * wide - r)], False)
        return c

    @pl.when(jnp.max(carry_ref[...]) > EXP2_IS_ZERO_BELOW)
    def _():
        lax.fori_loop(0, tiles, rest_of_tile, 0)

    for tile in range(tiles):
        for pair in range(hp // HEAD_PAIR):
            o_t = jnp.concatenate([acc_ref[tile, pair * HEAD_PAIR], acc_ref[tile, pair * HEAD_PAIR + 1]], axis=0)
            o_ref[tile * t:(tile + 1) * t, pair * LANES:(pair + 1) * LANES] = o_t.T.astype(o_ref.dtype)


def _sb_call(qs, ks, vst, *, t, hp, wide, tiles):
    B, S, _ = ks.shape
    r = lax.broadcasted_iota(jnp.int32, (t, t), 0)
    c = lax.broadcasted_iota(jnp.int32, (t, t), 1)
    tri = (c >= r).astype(BF16)
    w = hp * SB_DIM
    n_steps = S // (tiles * t)
    return pl.pallas_call(
        functools.partial(_sb_kernel, t=t, hp=hp, wide=wide, tiles=tiles),
        grid=(B, SB_HEADS // hp, n_steps),
        in_specs=[pl.BlockSpec((hp * LANES, tiles * t), lambda b, g, i: (g, b * n_steps + i)),
                  pl.BlockSpec((None, S, w), lambda b, g, i: (b, 0, g)),
                  pl.BlockSpec((w, S), lambda b, g, i: (g, b)),
                  pl.BlockSpec((t, t), lambda b, g, i: (0, 0))],
        out_specs=pl.BlockSpec((None, tiles * t, w), lambda b, g, i: (b, i, g)),
        out_shape=jax.ShapeDtypeStruct((B, S, SB_HEADS * SB_DIM), BF16),
        scratch_shapes=[pltpu.VMEM((tiles, hp, SB_DIM, t), F32),
                        pltpu.VMEM((tiles, hp, 1, t), F32),
                        pltpu.VMEM((2, wide, hp, t, t), BF16),
                        pltpu.VMEM((2, wide, hp, 1, t), F32)],
        compiler_params=pltpu.CompilerParams(dimension_semantics=("parallel", "parallel", "arbitrary"),
                                             vmem_limit_bytes=VMEM_LIMIT),
        name="sb_attn",
    )(qs, ks, vst, tri)


def _post_kernel(x_ref, oa_ref, ob_ref, p_ref, g_mix_ref, wga_ref, wgb_ref, wbra_ref, wbrb_ref, wout_ref,
                 g_ffn_ref, wfg_ref, wfu_ref, wfd_ref, wpg_ref, wpp_ref, g_ple_ref, g_fin_ref, out_ref,
                 *, ff_chunk, final_norm):
    x = x_ref[...]
    n = _rms(x, g_mix_ref[...]).astype(BF16)
    merged = (jax.nn.sigmoid(_dot(n, wga_ref[...])) * _dot(oa_ref[...], wbra_ref[...])
              + jax.nn.sigmoid(_dot(n, wgb_ref[...])) * _dot(ob_ref[...], wbrb_ref[...]))
    h = x + _dot(merged.astype(BF16), wout_ref[...])

    n2 = _rms(h, g_ffn_ref[...]).astype(BF16)
    d_ff = wfg_ref.shape[1]
    ff = None
    for c0 in range(0, d_ff, ff_chunk):
        c1 = min(c0 + ff_chunk, d_ff)
        g = _dot(n2, wfg_ref[:, c0:c1])
        u = _dot(n2, wfu_ref[:, c0:c1])
        part = _dot((g * jax.nn.sigmoid(g) * u).astype(BF16), wfd_ref[c0:c1, :])
        ff = part if ff is None else ff + part
    h = h + ff

    e = _rms(_dot(p_ref[...].astype(BF16), wpp_ref[...]), g_ple_ref[...])
    h = h + jax.nn.sigmoid(_dot(h.astype(BF16), wpg_ref[...])) * e
    out_ref[...] = _rms(h, g_fin_ref[...]) if final_norm else h


def _post_call(x2, oa, ob, p2, consts, *, tm, ff_chunk, final_norm):
    T, D = x2.shape
    row = lambda w: pl.BlockSpec((tm, w), lambda i: (i, 0))
    return pl.pallas_call(
        functools.partial(_post_kernel, ff_chunk=ff_chunk, final_norm=final_norm),
        grid=(T // tm,),
        in_specs=[row(D), row(oa.shape[1]), row(ob.shape[1]), row(p2.shape[1])]
                 + [_const_spec(c.shape) for c in consts],
        out_specs=row(D),
        out_shape=jax.ShapeDtypeStruct((T, D), F32),
        compiler_params=pltpu.CompilerParams(dimension_semantics=("parallel",),
                                             vmem_limit_bytes=VMEM_LIMIT),
        name="post",
    )(x2, oa, ob, p2, *consts)


def _rotate_half_cols(w):
    half = w.shape[-1] // 2
    return jnp.concatenate([-w[..., half:], w[..., :half]], axis=-1)


def _layer_weights(w_in, w_q_b, w_kv_b):
    d = w_in.shape[0]
    o = 0
    cols = []
    for wd in (MLA_Q_RANK, MLA_KV_RANK, MLA_ROPE, SB_HEADS * SB_DIM, SB_HEADS * SB_DIM, SB_HEADS * SB_DIM, d, d):
        cols.append(w_in[:, o:o + wd])
        o += wd
    w_cq, w_ckv, w_kpe, w_qs, w_ks, w_vs, w_ga, w_gb = cols

    kpe_tile = jnp.concatenate([jnp.zeros_like(w_in[:, :MLA_NOPE]), w_kpe, _rotate_half_cols(w_kpe)], axis=1)
    wa = jnp.concatenate([w_cq, w_ckv, kpe_tile], axis=1)
    wsb = jnp.concatenate([w_qs, w_ks, w_vs], axis=1)

    qb = w_q_b.reshape(MLA_Q_RANK, MLA_HEADS, MLA_NOPE + MLA_ROPE)
    wq = jnp.concatenate([qb, _rotate_half_cols(qb[..., MLA_NOPE:])], axis=-1).reshape(MLA_Q_RANK, MLA_HEADS * LANES)

    kvb = w_kv_b.reshape(MLA_KV_RANK, MLA_HEADS, MLA_NOPE + MLA_V)
    wk = jnp.pad(kvb[..., :MLA_NOPE], ((0, 0), (0, 0), (0, LANES - MLA_NOPE))).reshape(MLA_KV_RANK, MLA_HEADS * LANES)
    wv = kvb[..., MLA_NOPE:].reshape(MLA_KV_RANK, MLA_HEADS * MLA_V)
    bf = lambda t: t.astype(BF16)
    return tuple(map(bf, (wa, wsb, wq, wk, wv, w_ga, w_gb)))


def _rope_freq_row():
    inv_freq = 1.0 / (ROPE_THETA ** (jnp.arange(0, MLA_ROPE, 2, dtype=F32) / MLA_ROPE))
    both = jnp.concatenate([inv_freq, inv_freq])
    return jnp.pad(both, (MLA_NOPE, LANES - MLA_NOPE - MLA_ROPE)).reshape(1, LANES)


def _tiles(B, S):
    T = B * S
    tm_proj = min(1024, T)
    tm_post = min(512, T)
    t_sb = min(256, S)
    t_mla = min(512, S)
    heads_per_step = 8
    sb_wide = 2
    mla_wide = 1
    sb_tiles = 2
    return tm_proj, tm_post, t_sb, t_mla, heads_per_step, sb_wide, mla_wide, sb_tiles


def kernel(x, p, positions, g_mix, w_in, g_q_a, w_q_b, g_kv_a, w_kv_b, w_br_mla, w_br_sb, w_out, g_ffn,
           w_ffn_gate, w_ffn_up, w_ffn_down, w_ple_gate, w_ple_proj, g_ple, g_final):
    B, S, D = x.shape
    T = B * S
    depth = w_in.shape[0]
    tm_proj, tm_post, t_sb, t_mla, hp, sb_wide, mla_wide, sb_tiles = _tiles(B, S)
    d_ff = w_ffn_gate.shape[-1]
    ff_chunk = min(d_ff, 1024)
    row = lambda g: g.reshape(1, -1).astype(F32)
    bf = lambda t: t.astype(BF16)

    pos2 = positions.reshape(T, 1).astype(jnp.int32)
    freq = _rope_freq_row()
    h = x.reshape(T, D)
    for i in range(depth):
        wa, wsb, wq, wk, wv, w_ga, w_gb = _layer_weights(w_in[i], w_q_b[i], w_kv_b[i])
        qm, km, vm, qs, ks, vs = _proj_call(h, pos2, freq, row(g_mix[i]), wa, wsb, row(g_q_a[i]), wq,
                                            row(g_kv_a[i]), wk, wv, tm=tm_proj)
        shp = lambda t: t.reshape(B, S, t.shape[-1])
        o_a = _mla_call(qm, shp(km), vm, t=t_mla, hp=hp, wide=mla_wide).reshape(T, -1)
        o_b = _sb_call(qs, shp(ks), vs, t=t_sb, hp=hp, wide=sb_wide,
                       tiles=sb_tiles if (S // t_sb) % sb_tiles == 0 else 1).reshape(T, -1)
        consts = (row(g_mix[i]), w_ga, w_gb, bf(w_br_mla[i]), bf(w_br_sb[i]), bf(w_out[i]), row(g_ffn[i]),
                  bf(w_ffn_gate[i]), bf(w_ffn_up[i]), bf(w_ffn_down[i]), bf(w_ple_gate[i]), bf(w_ple_proj[i]),
                  row(g_ple[i]), row(g_final))
        h = _post_call(h, o_a, o_b, p[i].reshape(T, -1), consts, tm=tm_post, ff_chunk=ff_chunk,
                       final_norm=(i == depth - 1))
    return h.reshape(B, S, D)
```
